```python
import math
import jax, jax.numpy as jnp
from jax import lax
import numpy as np

D_MODEL = 2048
BATCH = 8
SEQ = 4096
DEPTH = 1

MIX_WIDTH = D_MODEL
HEAD_DIM = 128
RG_WIDTH = MIX_WIDTH // 2
RG_HEADS = RG_WIDTH // HEAD_DIM
CONV_WIDTH = 4
RG_C = 8.0
GM_WIDTH = MIX_WIDTH // 4
GM_GROUPS = GM_WIDTH // HEAD_DIM
CHUNK = 128
XA_WIDTH = MIX_WIDTH // 4
XA_HEADS = XA_WIDTH // HEAD_DIM
MEM_LEN = 256
IN_COLS = 2 * RG_WIDTH + 2 * GM_WIDTH + XA_WIDTH
N_EXPERTS = 32
TOP_K = 4
D_FF = D_MODEL
SWIGLU_LIMIT = 7.0
SWIGLU_ALPHA = 1.702
MOE_BLOCK = 512
EPS = 1e-6

kernel_name = "hybrid_rglru_gmlp_memxattn_moe"


def rms_norm(x, g):
    xf = x.astype(jnp.float32)
    y = xf * lax.rsqrt(jnp.mean(xf * xf, axis=-1, keepdims=True) + EPS)
    return (y * g.astype(jnp.float32)).astype(x.dtype)


def rglru_mixer(xr, gate, conv_w, conv_b, w_a, b_a, w_i, b_i, lam):
    B, T, _ = xr.shape
    xc = lax.conv_general_dilated(
        xr, conv_w[:, None, :], window_strides=(1,), padding=[(CONV_WIDTH - 1, 0)],
        dimension_numbers=("NWC", "WIO", "NWC"), feature_group_count=RG_WIDTH) + conv_b
    xb = xc.reshape(B, T, RG_HEADS, HEAD_DIM)
    r = jax.nn.sigmoid(jnp.einsum("bthi,hij->bthj", xb, w_a) + b_a).reshape(B, T, RG_WIDTH)
    i = jax.nn.sigmoid(jnp.einsum("bthi,hij->bthj", xb, w_i) + b_i).reshape(B, T, RG_WIDTH)
    log_a = -RG_C * r.astype(jnp.float32) * jax.nn.softplus(-lam.astype(jnp.float32))
    a = jnp.exp(log_a)
    mult = jnp.sqrt(-jnp.expm1(2.0 * log_a))
    u = mult * (i * xc).astype(jnp.float32)

    def combine(c1, c2):
        a1, b1 = c1
        a2, b2 = c2
        return a1 * a2, a2 * b1 + b2

    _, h = lax.associative_scan(combine, (a, u), axis=1)
    return jax.nn.gelu(gate) * h.astype(xr.dtype)


def gmlp_mixer(u, v, v_norm_g, w_s, b_s):
    B, T, _ = u.shape
    u = jax.nn.gelu(u)
    v = rms_norm(jax.nn.gelu(v), v_norm_g)
    vb = v.reshape(B, T // CHUNK, CHUNK, GM_GROUPS, HEAD_DIM)
    causal = jnp.tril(jnp.ones((CHUNK, CHUNK), dtype=w_s.dtype))
    w = w_s * causal[None]
    sv = jnp.einsum("gts,bcsgd->bctgd", w, vb) + b_s.T[None, None, :, :, None]
    return u * sv.reshape(B, T, GM_WIDTH)


def memory_xattn(q, mem_n, w_mk, w_mv):
    B, T, _ = q.shape
    M = mem_n.shape[1]
    k = (mem_n @ w_mk).reshape(B, M, XA_HEADS, HEAD_DIM)
    v = (mem_n @ w_mv).reshape(B, M, XA_HEADS, HEAD_DIM)
    qh = q.reshape(B, T, XA_HEADS, HEAD_DIM)
    s = jnp.einsum("bthd,bmhd->bhtm", qh, k).astype(jnp.float32) * (HEAD_DIM ** -0.5)
    p = jax.nn.softmax(s, axis=-1).astype(q.dtype)
    o = jnp.einsum("bhtm,bmhd->bthd", p, v)
    return o.reshape(B, T, XA_WIDTH)


def moe_ffn(xn, w_router, b_router, w_gate_up, b_gate_up, w_down, b_down):
    B, T, D = xn.shape
    N = B * T
    A = N * TOP_K
    xt = xn.reshape(N, D)
    logits = (xt @ w_router + b_router).astype(jnp.float32)
    top_vals, top_idx = lax.top_k(logits, TOP_K)
    gates = jax.nn.softmax(top_vals, axis=-1).astype(xn.dtype)

    per_expert = -(-A // N_EXPERTS)
    blk = min(MOE_BLOCK, max(8, 1 << (per_expert - 1).bit_length()))
    n_blocks = A // blk + N_EXPERTS
    flat_e = top_idx.reshape(-1)
    order = jnp.argsort(flat_e)
    sorted_e = flat_e[order]
    tok = order // TOP_K
    counts = jnp.bincount(flat_e, length=N_EXPERTS).astype(jnp.int32)
    grp_start = jnp.cumsum(counts) - counts
    padded = ((counts + blk - 1) // blk) * blk
    pad_end = jnp.cumsum(padded)
    pad_start = pad_end - padded
    dest = pad_start[sorted_e] + (jnp.arange(A, dtype=jnp.int32) - grp_start[sorted_e])
    block_expert = jnp.clip(
        jnp.searchsorted(pad_end, jnp.arange(n_blocks, dtype=jnp.int32) * blk, side="right"),
        0, N_EXPERTS - 1).astype(jnp.int32)
    xs_pad = jnp.zeros((n_blocks * blk, D), xn.dtype).at[dest].set(xt[tok])

    def expert_block(args):
        xb, e = args
        gu = xb @ w_gate_up[e] + b_gate_up[e]
        g = jnp.minimum(gu[:, 0::2], SWIGLU_LIMIT)
        up = jnp.clip(gu[:, 1::2], -SWIGLU_LIMIT, SWIGLU_LIMIT)
        act = (up + 1.0) * (g * jax.nn.sigmoid(SWIGLU_ALPHA * g))
        return act @ w_down[e] + b_down[e]

    ys_pad = lax.map(expert_block, (xs_pad.reshape(n_blocks, blk, D), block_expert))
    y_sorted = ys_pad.reshape(n_blocks * blk, D)[dest]
    y = jnp.zeros((A, D), xn.dtype).at[order].set(y_sorted).reshape(N, TOP_K, D)
    out = jnp.einsum("nkd,nk->nd", y, gates)
    return out.reshape(B, T, D)


def setup_inputs(seed: int = 0) -> dict:
    key = jax.random.key(seed)
    ks = jax.random.split(key, 32)
    f32 = jnp.float32
    L, D = DEPTH, D_MODEL

    def nrm(k, shape, scale):
        return jax.random.normal(k, shape, f32) * scale

    def gain(k, shape):
        return 1.0 + 0.02 * jax.random.normal(k, shape, f32)

    a_pow = jax.random.uniform(ks[10], (L, RG_WIDTH), f32, 0.9, 0.999)
    a_base = a_pow ** (1.0 / RG_C)
    lru_lambda = jnp.log(a_base) - jnp.log1p(-a_base)

    return {
        "x": jax.random.normal(ks[0], (BATCH, SEQ, D), f32),
        "mem": jax.random.normal(ks[1], (BATCH, MEM_LEN, D), f32),
        "ln_mix_g": gain(ks[2], (L, D)),
        "w_in": nrm(ks[3], (L, D, IN_COLS), D ** -0.5),
        "conv_w": nrm(ks[4], (L, CONV_WIDTH, RG_WIDTH), CONV_WIDTH ** -0.5),
        "conv_b": nrm(ks[5], (L, RG_WIDTH), 0.01),
        "w_rg_a": nrm(ks[6], (L, RG_HEADS, HEAD_DIM, HEAD_DIM), HEAD_DIM ** -0.5),
        "b_rg_a": nrm(ks[7], (L, RG_HEADS, HEAD_DIM), 0.01),
        "w_rg_i": nrm(ks[8], (L, RG_HEADS, HEAD_DIM, HEAD_DIM), HEAD_DIM ** -0.5),
        "b_rg_i": nrm(ks[9], (L, RG_HEADS, HEAD_DIM), 0.01),
        "lru_lambda": lru_lambda,
        "gm_v_norm_g": gain(ks[11], (L, GM_WIDTH)),
        "w_spatial": nrm(ks[12], (L, GM_GROUPS, CHUNK, CHUNK), CHUNK ** -0.5),
        "b_spatial": gain(ks[13], (L, GM_GROUPS, CHUNK)),
        "mem_norm_g": gain(ks[14], (L, D)),
        "w_mem_k": nrm(ks[15], (L, D, XA_WIDTH), D ** -0.5),
        "w_mem_v": nrm(ks[16], (L, D, XA_WIDTH), D ** -0.5),
        "out_norm_g": gain(ks[17], (L, MIX_WIDTH)),
        "w_out": nrm(ks[18], (L, MIX_WIDTH, D), MIX_WIDTH ** -0.5),
        "ln_ffn_g": gain(ks[19], (L, D)),
        "w_router": nrm(ks[20], (L, D, N_EXPERTS), D ** -0.5),
        "b_router": nrm(ks[21], (L, N_EXPERTS), 0.01),
        "w_gate_up": nrm(ks[22], (L, N_EXPERTS, D, 2 * D_FF), D ** -0.5),
        "b_gate_up": nrm(ks[23], (L, N_EXPERTS, 2 * D_FF), 0.01),
        "w_down": nrm(ks[24], (L, N_EXPERTS, D_FF, D), D_FF ** -0.5),
        "b_down": nrm(ks[25], (L, N_EXPERTS, D), 0.01),
        "final_norm_g": gain(ks[26], (D,)),
    }


def reference(x, mem, ln_mix_g, w_in, conv_w, conv_b, w_rg_a, b_rg_a, w_rg_i, b_rg_i,
              lru_lambda, gm_v_norm_g, w_spatial, b_spatial, mem_norm_g, w_mem_k, w_mem_v,
              out_norm_g, w_out, ln_ffn_g, w_router, b_router, w_gate_up, b_gate_up,
              w_down, b_down, final_norm_g):
    h = x
    c1 = RG_WIDTH
    c2 = 2 * RG_WIDTH
    c3 = c2 + GM_WIDTH
    c4 = c3 + GM_WIDTH
    for l in range(DEPTH):
        xn = rms_norm(h, ln_mix_g[l])
        p = xn @ w_in[l]
        y_rg = rglru_mixer(p[..., :c1], p[..., c1:c2], conv_w[l], conv_b[l],
                           w_rg_a[l], b_rg_a[l], w_rg_i[l], b_rg_i[l], lru_lambda[l])
        y_gm = gmlp_mixer(p[..., c2:c3], p[..., c3:c4], gm_v_norm_g[l], w_spatial[l], b_spatial[l])
        mem_n = rms_norm(mem, mem_norm_g[l])
        y_xa = memory_xattn(p[..., c4:], mem_n, w_mem_k[l], w_mem_v[l])
        g = out_norm_g[l]
        y = jnp.concatenate([
            rms_norm(y_rg, g[:RG_WIDTH]),
            rms_norm(y_gm, g[RG_WIDTH:RG_WIDTH + GM_WIDTH]),
            rms_norm(y_xa, g[RG_WIDTH + GM_WIDTH:]),
        ], axis=-1)
        h = h + y @ w_out[l]
        hn = rms_norm(h, ln_ffn_g[l])
        h = h + moe_ffn(hn, w_router[l], b_router[l], w_gate_up[l], b_gate_up[l],
                        w_down[l], b_down[l])
    return rms_norm(h, final_norm_g)
```

```python
import functools

import jax
import jax.numpy as jnp
from jax import lax
from jax.experimental import pallas as pl
from jax.experimental.pallas import tpu as pltpu

F32 = jnp.float32
BF16 = jnp.bfloat16
I32 = jnp.int32
U32 = jnp.uint32

HEAD_DIM = 128
RG_C = 8.0
TOP_K = 4
SWIGLU_LIMIT = 7.0
SWIGLU_ALPHA = 1.702
EPS = 1e-6

LANES = 128
SUBLANES = 8
MXU_DIM = 256
VMEM_LIMIT_BYTES = 56 * 1024 * 1024

MIX_TILE = 256
MOE_TILE = 512
UP_TN = 1024
DOWN_TN = 1024
DISPATCH_TILE = 512
DISPATCH_LAG = 64
COMBINE_TILE = 128


def _rms(x, g):
    return x * lax.rsqrt(jnp.mean(x * x, axis=-1, keepdims=True) + EPS) * g


def _gelu(x):
    return x * (0.5 * (1.0 + jnp.tanh(0.7978845608028654 * (x + 0.044715 * (x * x * x)))))


def _sigmoid(x):
    return 1.0 / (1.0 + jnp.exp(-x))


def _const_spec(shape):
    nd = len(shape)
    return pl.BlockSpec(shape, lambda *_: (0,) * nd, pipeline_mode=pl.Buffered(1))


def _mem_kv_kernel(mem_ref, g_ref, wk_ref, wv_ref, k_ref, v_ref):
    mn = _rms(mem_ref[0], g_ref[...]).astype(BF16)
    k_ref[0] = jnp.dot(mn, wk_ref[...], preferred_element_type=F32).astype(BF16)
    v_ref[0] = jnp.dot(mn, wv_ref[...], preferred_element_type=F32).astype(BF16)


def _mem_kv(mem, g, wk, wv):
    b, m, d = mem.shape
    xa = wk.shape[1]
    return pl.pallas_call(
        _mem_kv_kernel,
        grid=(b,),
        in_specs=[
            pl.BlockSpec((1, m, d), lambda i: (i, 0, 0)),
            _const_spec((1, d)),
            _const_spec((d, xa)),
            _const_spec((d, xa)),
        ],
        out_specs=[
            pl.BlockSpec((1, m, xa), lambda i: (i, 0, 0)),
            pl.BlockSpec((1, m, xa), lambda i: (i, 0, 0)),
        ],
        out_shape=[jax.ShapeDtypeStruct((b, m, xa), BF16), jax.ShapeDtypeStruct((b, m, xa), BF16)],
        compiler_params=pltpu.CompilerParams(
            dimension_semantics=("arbitrary",), vmem_limit_bytes=VMEM_LIMIT_BYTES),
        name="mem_kv",
    )(mem, g, wk, wv)


def _mixer_kernel(x_ref, lnmix_ref, win_ref, cw_ref, cb_ref, wa_ref, ba_ref, wi_ref, bi_ref, lam_ref,
                  gvn_ref, wsp_ref, bst_ref, k_ref, v_ref, outg_ref, wout_ref, lnffn_ref, wrt_ref, br_ref,
                  h_ref, hnp_ref, idx_ref, gate_ref, rank_ref, cnt_ref,
                  p_s, xr_s, a_s, u_s, hs_s, hc_s, y_s, yb_s, run_s,
                  *, rg, gm, xa, chunk):
    tt = x_ref.shape[1]
    d = x_ref.shape[2]
    n_exp = wrt_ref.shape[0]
    c1, c2, c3, c4 = rg, 2 * rg, 2 * rg + gm, 2 * rg + 2 * gm
    bi_ = pl.program_id(0)
    ti_ = pl.program_id(1)

    @pl.when(ti_ == 0)
    def _():
        xr_s[0:SUBLANES, :] = jnp.zeros((SUBLANES, rg), F32)
        hc_s[...] = jnp.zeros((SUBLANES, rg), F32)

    @pl.when((ti_ == 0) & (bi_ == 0))
    def _():
        run_s[...] = jnp.zeros(run_s.shape, F32)

    x = x_ref[0]
    xn = _rms(x, lnmix_ref[...]).astype(BF16)
    p_s[...] = jnp.dot(xn, win_ref[...], preferred_element_type=F32)

    xr_s[SUBLANES:SUBLANES + tt, :] = p_s[:, 0:c1]
    cw = cw_ref[...]
    kw = cw.shape[0]
    xc = cb_ref[...] + cw[0:1, :] * xr_s[pl.ds(SUBLANES - kw + 1, tt), :]
    for w in range(1, kw):
        xc = xc + cw[w:w + 1, :] * xr_s[pl.ds(SUBLANES - kw + 1 + w, tt), :]
    xr_s[0:SUBLANES, :] = xr_s[tt:tt + SUBLANES, :]

    z = -lam_ref[...]
    sp = jnp.maximum(z, 0.0) + jnp.log1p(jnp.exp(-jnp.abs(z)))
    for hd in range(rg // HEAD_DIM):
        sl = slice(hd * HEAD_DIM, (hd + 1) * HEAD_DIM)
        xch = xc[:, sl]
        xb = xch.astype(BF16)
        r = _sigmoid(jnp.dot(xb, wa_ref[hd], preferred_element_type=F32) + ba_ref[:, sl])
        ig = _sigmoid(jnp.dot(xb, wi_ref[hd], preferred_element_type=F32) + bi_ref[:, sl])
        log_a = (-RG_C) * r * sp[:, sl]
        a = jnp.exp(log_a)
        a_s[:, sl] = a
        u_s[:, sl] = jnp.sqrt(-jnp.tanh(log_a) * (a * a + 1.0)) * (ig * xch)

    row = lax.broadcasted_iota(I32, (SUBLANES, rg), 0)

    def scan_group(g, hprev):
        r0 = pl.multiple_of(g * SUBLANES, SUBLANES)
        aa = a_s[pl.ds(r0, SUBLANES), :]
        bb = u_s[pl.ds(r0, SUBLANES), :]
        for s in (1, 2, 4):
            keep = row >= s
            bb = jnp.where(keep, aa * pltpu.roll(bb, s, 0) + bb, bb)
            aa = jnp.where(keep, aa * pltpu.roll(aa, s, 0), aa)
        h8 = aa * hprev + bb
        hs_s[pl.ds(r0, SUBLANES), :] = h8
        return jnp.broadcast_to(h8[SUBLANES - 1:SUBLANES, :], (SUBLANES, rg))

    hc_s[...] = lax.fori_loop(0, tt // SUBLANES, scan_group, hc_s[...])
    y_s[:, 0:rg] = _gelu(p_s[:, c1:c2]) * hs_s[...]

    vn = _rms(_gelu(p_s[:, c3:c4]), gvn_ref[...]).astype(BF16)
    tri = (lax.broadcasted_iota(I32, (chunk, chunk), 0) >= lax.broadcasted_iota(I32, (chunk, chunk), 1))
    for g in range(gm // HEAD_DIM):
        wg = jnp.where(tri, wsp_ref[g], 0.0).astype(BF16)
        bcol = bst_ref[:, g:g + 1]
        for c in range(tt // chunk):
            rs = slice(c * chunk, (c + 1) * chunk)
            cs = slice(g * HEAD_DIM, (g + 1) * HEAD_DIM)
            sv = jnp.dot(wg, vn[rs, cs], preferred_element_type=F32) + bcol
            y_s[rs, rg + g * HEAD_DIM:rg + (g + 1) * HEAD_DIM] = (
                _gelu(p_s[rs, c2 + g * HEAD_DIM:c2 + (g + 1) * HEAD_DIM]) * sv)

    scale = HEAD_DIM ** -0.5
    for hd in range(xa // HEAD_DIM):
        sl = slice(hd * HEAD_DIM, (hd + 1) * HEAD_DIM)
        q = p_s[:, c4 + hd * HEAD_DIM:c4 + (hd + 1) * HEAD_DIM].astype(BF16)
        s = lax.dot_general(q, k_ref[0, :, sl], (((1,), (1,)), ((), ())),
                            preferred_element_type=F32) * scale
        e = jnp.exp(s - jnp.max(s, axis=-1, keepdims=True))
        o = jnp.dot(e.astype(BF16), v_ref[0, :, sl], preferred_element_type=F32)
        y_s[:, rg + gm + hd * HEAD_DIM:rg + gm + (hd + 1) * HEAD_DIM] = o / jnp.sum(e, axis=-1, keepdims=True)

    for lo, hi in ((0, rg), (rg, rg + gm), (rg + gm, rg + gm + xa)):
        yb_s[:, lo:hi] = _rms(y_s[:, lo:hi], outg_ref[:, lo:hi]).astype(BF16)
    h = x + jnp.dot(yb_s[...], wout_ref[...], preferred_element_type=F32)
    h_ref[0] = h

    hn = _rms(h, lnffn_ref[...])
    half = d // 2
    for c in range(half // LANES):
        lo = hn[:, c * LANES:(c + 1) * LANES].astype(BF16).astype(F32)
        hi = hn[:, half + c * LANES:half + (c + 1) * LANES].astype(BF16).astype(F32)
        word = (pltpu.bitcast(lo, U32) >> 16) | (pltpu.bitcast(hi, U32) & jnp.uint32(0xFFFF0000))
        hnp_ref[:, c, :] = word

    lg = lax.dot_general(wrt_ref[...], hn, (((1,), (1,)), ((), ())),
                         precision=lax.Precision.HIGHEST, preferred_element_type=F32) + br_ref[...]
    eid = lax.broadcasted_iota(I32, (n_exp, tt), 0)
    vals, sels = [], []
    for k in range(TOP_K):
        m = jnp.max(lg, axis=0, keepdims=True)
        ik = jnp.min(jnp.where(lg == m, eid, n_exp), axis=0, keepdims=True)
        sel = eid == ik
        vals.append(m)
        sels.append(sel)
        idx_ref[k:k + 1, :] = ik
        lg = jnp.where(sel, -jnp.inf, lg)
    es = [jnp.exp(v - vals[0]) for v in vals]
    den = es[0] + es[1] + es[2] + es[3]
    for k in range(TOP_K):
        gate_ref[k:k + 1, :] = es[k] / den
    oh = jnp.where(sels[0] | sels[1] | sels[2] | sels[3], 1.0, 0.0)
    upper = jnp.where(lax.broadcasted_iota(I32, (tt, tt), 0) < lax.broadcasted_iota(I32, (tt, tt), 1),
                      1.0, 0.0).astype(BF16)
    before = jnp.dot(oh.astype(BF16), upper, preferred_element_type=F32) + run_s[:, 0:1]
    for k in range(TOP_K):
        rank_ref[k:k + 1, :] = jnp.sum(jnp.where(sels[k], before, 0.0), axis=0, keepdims=True).astype(I32)
    run_s[...] = run_s[...] + jnp.sum(oh, axis=1, keepdims=True)
    cnt_ref[...] = run_s[...]


def _mixer(x, k, v, prm, *, tt):
    b, t, d = x.shape
    n_t = t // tt
    n = b * t
    rg = prm["conv_w"].shape[1]
    gm = prm["gvn"].shape[1]
    m = k.shape[1]
    xa = k.shape[2]
    chunk = prm["wsp"].shape[1]
    in_cols = prm["w_in"].shape[1]
    n_exp = prm["wrt"].shape[0]
    s_rows = d // (2 * LANES)
    const_names = ("lnmix", "w_in", "conv_w", "conv_b", "wa", "ba", "wi", "bi", "lam", "gvn", "wsp", "bst")
    const_names2 = ("outg", "w_out", "lnffn", "wrt", "br")
    in_specs = ([pl.BlockSpec((1, tt, d), lambda bi, ti: (bi, ti, 0))]
                + [_const_spec(prm[k].shape) for k in const_names]
                + [pl.BlockSpec((1, m, xa), lambda bi, ti: (bi, 0, 0)),
                   pl.BlockSpec((1, m, xa), lambda bi, ti: (bi, 0, 0))]
                + [_const_spec(prm[k].shape) for k in const_names2])
    tok = lambda bi, ti: (0, bi * n_t + ti)
    out_specs = [
        pl.BlockSpec((1, tt, d), lambda bi, ti: (bi, ti, 0)),
        pl.BlockSpec((tt, s_rows, LANES), lambda bi, ti: (bi * n_t + ti, 0, 0)),
        pl.BlockSpec((TOP_K, tt), tok),
        pl.BlockSpec((TOP_K, tt), tok),
        pl.BlockSpec((TOP_K, tt), tok),
        pl.BlockSpec((n_exp, LANES), lambda bi, ti: (0, 0)),
    ]
    out_shape = [
        jax.ShapeDtypeStruct((b, t, d), F32),
        jax.ShapeDtypeStruct((n, s_rows, LANES), U32),
        jax.ShapeDtypeStruct((TOP_K, n), I32),
        jax.ShapeDtypeStruct((TOP_K, n), F32),
        jax.ShapeDtypeStruct((TOP_K, n), I32),
        jax.ShapeDtypeStruct((n_exp, LANES), F32),
    ]
    scratch = [
        pltpu.VMEM((tt, in_cols), F32),
        pltpu.VMEM((tt + 2 * SUBLANES, rg), F32),
        pltpu.VMEM((tt, rg), F32),
        pltpu.VMEM((tt, rg), F32),
        pltpu.VMEM((tt, rg), F32),
        pltpu.VMEM((SUBLANES, rg), F32),
        pltpu.VMEM((tt, rg + gm + xa), F32),
        pltpu.VMEM((tt, rg + gm + xa), BF16),
        pltpu.VMEM((n_exp, LANES), F32),
    ]
    return pl.pallas_call(
        functools.partial(_mixer_kernel, rg=rg, gm=gm, xa=xa, chunk=chunk),
        grid=(b, n_t),
        in_specs=in_specs,
        out_specs=out_specs,
        out_shape=out_shape,
        scratch_shapes=scratch,
        compiler_params=pltpu.CompilerParams(
            dimension_semantics=("arbitrary", "arbitrary"), vmem_limit_bytes=VMEM_LIMIT_BYTES),
        name="mixer",
    )(x, *[prm[c] for c in const_names], k, v, *[prm[c] for c in const_names2])


def _dispatch_kernel(zs_ref, zl_ref, na_ref, dest_ref, hnp_ref, xs_ref, zb, sem, zsem, *, tile, lag, tm):
    step = pl.program_id(0)
    n_exp = zs_ref.shape[0]
    nb = xs_ref.shape[0] // tm

    def for_each_zero_copy(act):
        def per_expert(e, carry):
            off = zs_ref[e]
            length = zl_ref[e]
            p = tm // 2
            while p >= 1:
                bit = length & p

                @pl.when(bit != 0)
                def _(off=off, p=p):
                    act(pltpu.make_async_copy(zb.at[pl.ds(0, p)], xs_ref.at[pl.ds(off, p)], zsem))
                off = off + bit
                p //= 2
            return carry
        lax.fori_loop(0, n_exp, per_expert, 0)

        def per_block(b, carry):
            act(pltpu.make_async_copy(zb, xs_ref.at[pl.ds(pl.multiple_of(b * tm, tm), tm)], zsem))
            return carry
        lax.fori_loop(na_ref[0], nb, per_block, 0)

    @pl.when(step == 0)
    def _():
        zb[...] = jnp.zeros(zb.shape, U32)
        for_each_zero_copy(lambda c: c.start())

    def row_copy(n, slot):
        return pltpu.make_async_copy(hnp_ref.at[n], xs_ref.at[slot], sem)

    def wait_token():
        for _ in range(TOP_K):
            row_copy(0, 0).wait()

    def body(n, carry):
        for k in range(TOP_K):
            row_copy(n, dest_ref[k * tile + n]).start()

        @pl.when(n >= lag)
        def _():
            wait_token()
        return carry

    lax.fori_loop(0, tile, body, 0)

    def drain(_, carry):
        wait_token()
        return carry

    lax.fori_loop(0, lag, drain, 0)

    @pl.when(step == 0)
    def _():
        for_each_zero_copy(lambda c: c.wait())


def _dispatch(zero_start, zero_len, n_active, dest_tiles, hnp, rows, *, tile, tm):
    n, s_rows, _ = hnp.shape
    lag = min(DISPATCH_LAG, tile)
    grid_spec = pltpu.PrefetchScalarGridSpec(
        num_scalar_prefetch=3,
        grid=(n // tile,),
        in_specs=[
            pl.BlockSpec((TOP_K * tile,), lambda i, *_: (i,), memory_space=pltpu.SMEM),
            pl.BlockSpec((tile, s_rows, LANES), lambda i, *_: (i, 0, 0)),
        ],
        out_specs=pl.BlockSpec(memory_space=pl.ANY),
        scratch_shapes=[pltpu.VMEM((tm, s_rows, LANES), U32), pltpu.SemaphoreType.DMA(()),
                        pltpu.SemaphoreType.DMA(())],
    )
    return pl.pallas_call(
        functools.partial(_dispatch_kernel, tile=tile, lag=lag, tm=tm),
        grid_spec=grid_spec,
        out_shape=jax.ShapeDtypeStruct((rows, s_rows, LANES), U32),
        compiler_params=pltpu.CompilerParams(
            dimension_semantics=("arbitrary",), vmem_limit_bytes=VMEM_LIMIT_BYTES,
            disable_bounds_checks=True, has_side_effects=True),
        name="dispatch",
    )(zero_start, zero_len, n_active, dest_tiles, hnp)


def _expert_changed(be_ref, i):
    prev = be_ref[jnp.maximum(i - 1, 0)]
    return (i == 0) | (be_ref[i] != prev)


def _moe_up_kernel(be_ref, na_ref, xs_ref, w_ref, bg_ref, bu_ref, perm_ref, act_ref, wp_s, xb_s):
    i = pl.program_id(1)
    tn = w_ref.shape[2]
    hn_ = tn // 2
    d = xb_s.shape[1]
    active = i < na_ref[0]

    @pl.when(active & _expert_changed(be_ref, i))
    def _():
        for g in range(tn // MXU_DIM):
            wg = w_ref[0, :, g * MXU_DIM:(g + 1) * MXU_DIM].astype(BF16)
            wq = jnp.dot(wg, perm_ref[...], preferred_element_type=F32).astype(BF16)
            hw = MXU_DIM // 2
            wp_s[:, g * hw:(g + 1) * hw] = wq[:, 0:hw]
            wp_s[:, hn_ + g * hw:hn_ + (g + 1) * hw] = wq[:, hw:MXU_DIM]

    @pl.when(active)
    def _():
        half = d // 2
        for c in range(half // LANES):
            word = xs_ref[:, c, :]
            xb_s[:, c * LANES:(c + 1) * LANES] = pltpu.bitcast(word << 16, F32).astype(BF16)
            xb_s[:, half + c * LANES:half + (c + 1) * LANES] = (
                pltpu.bitcast(word & jnp.uint32(0xFFFF0000), F32).astype(BF16))
        gu = jnp.dot(xb_s[...], wp_s[...], preferred_element_type=F32)
        gate = jnp.minimum(gu[:, 0:hn_] + bg_ref[0], SWIGLU_LIMIT)
        up = jnp.clip(gu[:, hn_:tn] + bu_ref[0], -SWIGLU_LIMIT, SWIGLU_LIMIT)
        act_ref[...] = ((up + 1.0) * (gate * _sigmoid(SWIGLU_ALPHA * gate))).astype(BF16)

    @pl.when(jnp.logical_not(active))
    def _():
        act_ref[...] = jnp.zeros(act_ref.shape, BF16)


def _moe_up(block_expert, n_active, xs, w_gate_up, bg, bu, perm, *, tm, tn):
    rows, s_rows, _ = xs.shape
    n_exp, d, f2 = w_gate_up.shape
    nb = rows // tm

    def blk(i, na):
        return jnp.minimum(i, na[0] - 1)

    grid_spec = pltpu.PrefetchScalarGridSpec(
        num_scalar_prefetch=2,
        grid=(f2 // tn, nb),
        in_specs=[
            pl.BlockSpec((tm, s_rows, LANES), lambda j, i, be, na: (blk(i, na), 0, 0)),
            pl.BlockSpec((1, d, tn), lambda j, i, be, na: (be[blk(i, na)], 0, j)),
            pl.BlockSpec((1, 1, tn // 2), lambda j, i, be, na: (be[blk(i, na)], 0, j)),
            pl.BlockSpec((1, 1, tn // 2), lambda j, i, be, na: (be[blk(i, na)], 0, j)),
            pl.BlockSpec((MXU_DIM, MXU_DIM), lambda j, i, be, na: (0, 0)),
        ],
        out_specs=pl.BlockSpec((tm, tn // 2), lambda j, i, be, na: (i, j)),
        scratch_shapes=[pltpu.VMEM((d, tn), BF16), pltpu.VMEM((tm, d), BF16)],
    )
    return pl.pallas_call(
        _moe_up_kernel,
        grid_spec=grid_spec,
        out_shape=jax.ShapeDtypeStruct((rows, f2 // 2), BF16),
        compiler_params=pltpu.CompilerParams(
            dimension_semantics=("arbitrary", "arbitrary"), vmem_limit_bytes=VMEM_LIMIT_BYTES),
        name="moe_up",
    )(block_expert, n_active, xs, w_gate_up, bg, bu, perm)


def _moe_down_kernel(be_ref, na_ref, act_ref, w_ref, b_ref, ys_ref, wb_s):
    i = pl.program_id(1)
    active = i < na_ref[0]

    @pl.when(active & _expert_changed(be_ref, i))
    def _():
        wb_s[...] = w_ref[0].astype(BF16)

    @pl.when(active)
    def _():
        y = jnp.dot(act_ref[...], wb_s[...], preferred_element_type=F32) + b_ref[0]
        for c in range(ys_ref.shape[1]):
            ys_ref[:, c, :] = y[:, c * LANES:(c + 1) * LANES]

    @pl.when(jnp.logical_not(active))
    def _():
        ys_ref[...] = jnp.zeros(ys_ref.shape, F32)


def _moe_down(block_expert, n_active, act, w_down, b_down, *, tm, tn):
    rows, f = act.shape
    n_exp, _, d = w_down.shape
    nb = rows // tm

    def blk(i, na):
        return jnp.minimum(i, na[0] - 1)

    grid_spec = pltpu.PrefetchScalarGridSpec(
        num_scalar_prefetch=2,
        grid=(d // tn, nb),
        in_specs=[
            pl.BlockSpec((tm, f), lambda j, i, be, na: (blk(i, na), 0)),
            pl.BlockSpec((1, f, tn), lambda j, i, be, na: (be[blk(i, na)], 0, j)),
            pl.BlockSpec((1, 1, tn), lambda j, i, be, na: (be[blk(i, na)], 0, j)),
        ],
        out_specs=pl.BlockSpec((tm, tn // LANES, LANES), lambda j, i, be, na: (i, j, 0)),
        scratch_shapes=[pltpu.VMEM((f, tn), BF16)],
    )
    return pl.pallas_call(
        _moe_down_kernel,
        grid_spec=grid_spec,
        out_shape=jax.ShapeDtypeStruct((rows, d // LANES, LANES), F32),
        compiler_params=pltpu.CompilerParams(
            dimension_semantics=("arbitrary", "arbitrary"), vmem_limit_bytes=VMEM_LIMIT_BYTES),
        name="moe_down",
    )(block_expert, n_active, act, w_down, b_down)


def _combine_kernel(dcur_ref, dnext_ref, ys_ref, h_ref, gates_ref, fg_ref, out_ref, buf, sem, *, tile):
    s = pl.program_id(0)
    n_steps = pl.num_programs(0)
    n_chunks = buf.shape[3]

    def row_copy(slot, k, n, src_row):
        return pltpu.make_async_copy(ys_ref.at[src_row], buf.at[slot, k, n], sem.at[slot])

    def issue(d_ref, slot):
        def body(n, carry):
            for k in range(TOP_K):
                row_copy(slot, k, n, d_ref[k * tile + n]).start()
            return carry
        lax.fori_loop(0, tile, body, 0)

    @pl.when(s == 0)
    def _():
        issue(dcur_ref, 0)

    @pl.when(s + 1 < n_steps)
    def _():
        issue(dnext_ref, (s + 1) % 2)

    def consume(slot):
        def wait_body(n, carry):
            for k in range(TOP_K):
                row_copy(slot, k, 0, 0).wait()
            return carry
        lax.fori_loop(0, tile, wait_body, 0)
        gs = [gates_ref[:, k:k + 1] for k in range(TOP_K)]
        ssq = jnp.zeros((tile, 1), F32)
        for c in range(n_chunks):
            cs = slice(c * LANES, (c + 1) * LANES)
            acc = h_ref[:, cs]
            for k in range(TOP_K):
                acc = acc + gs[k] * buf[slot, k, :, c, :]
            out_ref[:, cs] = acc
            ssq = ssq + jnp.sum(acc * acc, axis=-1, keepdims=True)
        d = out_ref.shape[1]
        out_ref[...] = out_ref[...] * lax.rsqrt(ssq / d + EPS) * fg_ref[...]

    for slot in range(2):
        @pl.when(s % 2 == slot)
        def _(slot=slot):
            consume(slot)


def _combine(dest_tiles, ys, h2, gates_t, fg, *, tile):
    n, d = h2.shape
    n_chunks = ys.shape[1]
    n_steps = n // tile
    return pl.pallas_call(
        functools.partial(_combine_kernel, tile=tile),
        grid=(n_steps,),
        in_specs=[
            pl.BlockSpec((TOP_K * tile,), lambda i: (i,), memory_space=pltpu.SMEM),
            pl.BlockSpec((TOP_K * tile,), lambda i: (jnp.minimum(i + 1, n_steps - 1),), memory_space=pltpu.SMEM),
            pl.BlockSpec(memory_space=pl.ANY),
            pl.BlockSpec((tile, d), lambda i: (i, 0)),
            pl.BlockSpec((tile, TOP_K), lambda i: (i, 0)),
            _const_spec((1, d)),
        ],
        out_specs=pl.BlockSpec((tile, d), lambda i: (i, 0)),
        out_shape=jax.ShapeDtypeStruct((n, d), F32),
        scratch_shapes=[pltpu.VMEM((2, TOP_K, tile, n_chunks, LANES), F32), pltpu.SemaphoreType.DMA((2,))],
        compiler_params=pltpu.CompilerParams(
            dimension_semantics=("arbitrary",), vmem_limit_bytes=VMEM_LIMIT_BYTES,
            disable_bounds_checks=True),
        name="combine",
    )(dest_tiles, dest_tiles, ys, h2, gates_t, fg)


def _tile_major(a, tile):
    k, n = a.shape
    return a.reshape(k, n // tile, tile).transpose(1, 0, 2).reshape(-1)


def _layer(h, mem, ln_mix_g, w_in, conv_w, conv_b, w_rg_a, b_rg_a, w_rg_i, b_rg_i, lru_lambda, gm_v_norm_g,
           w_spatial, b_spatial, mem_norm_g, w_mem_k, w_mem_v, out_norm_g, w_out, ln_ffn_g, w_router, b_router,
           w_gate_up, b_gate_up, w_down, b_down, final_g, *, mix_tile, moe_tile, up_tn, down_tn,
           dispatch_tile, combine_tile):
    b, t, d = h.shape
    n = b * t
    n_exp = w_router.shape[1]
    row = lambda a: a.reshape(1, -1)
    km, v = _mem_kv(mem, row(mem_norm_g), w_mem_k.astype(BF16), w_mem_v.astype(BF16))
    prm = dict(
        lnmix=row(ln_mix_g), w_in=w_in.astype(BF16), conv_w=conv_w, conv_b=row(conv_b),
        wa=w_rg_a.astype(BF16), ba=row(b_rg_a), wi=w_rg_i.astype(BF16), bi=row(b_rg_i), lam=row(lru_lambda),
        gvn=row(gm_v_norm_g), wsp=w_spatial, bst=b_spatial.T, outg=row(out_norm_g), w_out=w_out.astype(BF16),
        lnffn=row(ln_ffn_g), wrt=w_router.T, br=b_router.reshape(-1, 1))
    h1, hnp, idx, gates, rank, cnt = _mixer(h, km, v, prm, tt=mix_tile)

    counts = cnt[:, 0].astype(I32)
    padded = ((counts + moe_tile - 1) // moe_tile) * moe_tile
    pad_end = jnp.cumsum(padded)
    pad_start = pad_end - padded
    dest = pad_start[idx] + rank
    nb = (n * TOP_K) // moe_tile + n_exp
    n_active = (pad_end[-1] // moe_tile).astype(I32).reshape(1)
    block_expert = jnp.clip(
        jnp.searchsorted(pad_end, jnp.arange(nb, dtype=I32) * moe_tile, side="right"), 0, n_exp - 1).astype(I32)

    xs = _dispatch(pad_start + counts, padded - counts, n_active, _tile_major(dest, dispatch_tile), hnp,
                   nb * moe_tile, tile=dispatch_tile, tm=moe_tile)
    f = w_down.shape[1]
    r_ = jnp.arange(MXU_DIM)
    src = jnp.where(r_ < MXU_DIM // 2, 2 * r_, 2 * (r_ - MXU_DIM // 2) + 1)
    perm = (jnp.arange(MXU_DIM)[:, None] == src[None, :]).astype(BF16)
    bg = b_gate_up[:, 0::2].reshape(n_exp, 1, f)
    bu = b_gate_up[:, 1::2].reshape(n_exp, 1, f)
    act = _moe_up(block_expert, n_active, xs, w_gate_up, bg, bu, perm, tm=moe_tile, tn=up_tn)
    ys = _moe_down(block_expert, n_active, act, w_down, b_down.reshape(n_exp, 1, d), tm=moe_tile, tn=down_tn)
    out = _combine(_tile_major(dest, combine_tile), ys, h1.reshape(n, d), gates.T, row(final_g),
                   tile=combine_tile)
    return out.reshape(b, t, d)


def kernel(x, mem, ln_mix_g, w_in, conv_w, conv_b, w_rg_a, b_rg_a, w_rg_i, b_rg_i, lru_lambda, gm_v_norm_g, w_spatial, b_spatial, mem_norm_g, w_mem_k, w_mem_v, out_norm_g, w_out, ln_ffn_g, w_router, b_router, w_gate_up, b_gate_up, w_down, b_down, final_norm_g):
    depth = w_in.shape[0]
    assert depth == 1, "the final RMSNorm is fused into the single layer's combine stage"
    return _layer(
        x, mem, ln_mix_g[0], w_in[0], conv_w[0], conv_b[0], w_rg_a[0], b_rg_a[0], w_rg_i[0], b_rg_i[0],
        lru_lambda[0], gm_v_norm_g[0], w_spatial[0], b_spatial[0], mem_norm_g[0], w_mem_k[0], w_mem_v[0],
        out_norm_g[0], w_out[0], ln_ffn_g[0], w_router[0], b_router[0], w_gate_up[0], b_gate_up[0], w_down[0],
        b_down[0], final_norm_g,
        mix_tile=MIX_TILE, moe_tile=MOE_TILE, up_tn=UP_TN, down_tn=DOWN_TN,
        dispatch_tile=DISPATCH_TILE, combine_tile=COMBINE_TILE)
```

```python
import functools

import jax
import jax.numpy as jnp
from jax import lax
from jax.experimental import pallas as pl
from jax.experimental.pallas import tpu as pltpu

F32 = jnp.float32
BF16 = jnp.bfloat16
I32 = jnp.int32
U32 = jnp.uint32

HEAD_DIM = 128
RG_C = 8.0
TOP_K = 4
SWIGLU_LIMIT = 7.0
SWIGLU_ALPHA = 1.702
EPS = 1e-6

LANES = 128
SUBLANES = 8
MXU_DIM = 256
VMEM_LIMIT_BYTES = 56 * 1024 * 1024

MIX_TILE = 256
MOE_TILE = 512
UP_TN = 1024
DOWN_ROWS = 256
DISPATCH_TILE = 512
DISPATCH_LAG = 64
COMBINE_TILE = 256


def _rms(x, g):
    return x * lax.rsqrt(jnp.mean(x * x, axis=-1, keepdims=True) + EPS) * g


def _gelu(x):
    return x * (0.5 * (1.0 + jnp.tanh(0.7978845608028654 * (x + 0.044715 * (x * x * x)))))


def _sigmoid(x):
    return 1.0 / (1.0 + jnp.exp(-x))


def _const_spec(shape):
    nd = len(shape)
    return pl.BlockSpec(shape, lambda *_: (0,) * nd, pipeline_mode=pl.Buffered(1))


def _store_packed_rows(ref, y, s_rows):
    rows, d = y.shape
    half = d // 2
    for c in range(s_rows):
        lo = y[:, c * LANES:(c + 1) * LANES].astype(BF16).astype(F32)
        hi = y[:, half + c * LANES:half + (c + 1) * LANES].astype(BF16).astype(F32)
        word = (pltpu.bitcast(lo, U32) >> 16) | (pltpu.bitcast(hi, U32) & jnp.uint32(0xFFFF0000))
        ref[pl.ds(c, rows, stride=s_rows), :] = word


def _load_packed_rows(ref, c, rows, s_rows):
    word = ref[pl.ds(c, rows, stride=s_rows), :]
    return pltpu.bitcast(word << 16, F32), pltpu.bitcast(word & jnp.uint32(0xFFFF0000), F32)


def _mem_kv_kernel(mem_ref, g_ref, wk_ref, wv_ref, k_ref, v_ref):
    mn = _rms(mem_ref[0], g_ref[...]).astype(BF16)
    k_ref[0] = jnp.dot(mn, wk_ref[...], preferred_element_type=F32).astype(BF16)
    v_ref[0] = jnp.dot(mn, wv_ref[...], preferred_element_type=F32).astype(BF16)


def _mem_kv(mem, g, wk, wv):
    b, m, d = mem.shape
    xa = wk.shape[1]
    return pl.pallas_call(
        _mem_kv_kernel,
        grid=(b,),
        in_specs=[
            pl.BlockSpec((1, m, d), lambda i: (i, 0, 0)),
            _const_spec((1, d)),
            _const_spec((d, xa)),
            _const_spec((d, xa)),
        ],
        out_specs=[
            pl.BlockSpec((1, m, xa), lambda i: (i, 0, 0)),
            pl.BlockSpec((1, m, xa), lambda i: (i, 0, 0)),
        ],
        out_shape=[jax.ShapeDtypeStruct((b, m, xa), BF16), jax.ShapeDtypeStruct((b, m, xa), BF16)],
        compiler_params=pltpu.CompilerParams(
            dimension_semantics=("arbitrary",), vmem_limit_bytes=VMEM_LIMIT_BYTES),
        name="mem_kv",
    )(mem, g, wk, wv)


def _mixer_kernel(x_ref, lnmix_ref, win_ref, cw_ref, cb_ref, wa_ref, ba_ref, wi_ref, bi_ref, lam_ref,
                  gvn_ref, wsp_ref, bst_ref, k_ref, v_ref, outg_ref, wout_ref, lnffn_ref, wrt_ref, br_ref,
                  h_ref, hnp_ref, idx_ref, gate_ref, rank_ref, cnt_ref,
                  p_s, xr_s, a_s, u_s, hs_s, hc_s, y_s, yb_s, run_s,
                  *, rg, gm, xa, chunk):
    tt = x_ref.shape[1]
    d = x_ref.shape[2]
    n_exp = wrt_ref.shape[0]
    c1, c2, c3, c4 = rg, 2 * rg, 2 * rg + gm, 2 * rg + 2 * gm
    bi_ = pl.program_id(0)
    ti_ = pl.program_id(1)

    @pl.when(ti_ == 0)
    def _():
        xr_s[0:SUBLANES, :] = jnp.zeros((SUBLANES, rg), F32)
        hc_s[...] = jnp.zeros((SUBLANES, rg), F32)

    @pl.when((ti_ == 0) & (bi_ == 0))
    def _():
        run_s[...] = jnp.zeros(run_s.shape, F32)

    x = x_ref[0]
    xn = _rms(x, lnmix_ref[...]).astype(BF16)
    p_s[...] = jnp.dot(xn, win_ref[...], preferred_element_type=F32)

    xr_s[SUBLANES:SUBLANES + tt, :] = p_s[:, 0:c1]
    cw = cw_ref[...]
    kw = cw.shape[0]
    xc = cb_ref[...] + cw[0:1, :] * xr_s[pl.ds(SUBLANES - kw + 1, tt), :]
    for w in range(1, kw):
        xc = xc + cw[w:w + 1, :] * xr_s[pl.ds(SUBLANES - kw + 1 + w, tt), :]
    xr_s[0:SUBLANES, :] = xr_s[tt:tt + SUBLANES, :]

    z = -lam_ref[...]
    sp = jnp.maximum(z, 0.0) + jnp.log1p(jnp.exp(-jnp.abs(z)))
    for hd in range(rg // HEAD_DIM):
        sl = slice(hd * HEAD_DIM, (hd + 1) * HEAD_DIM)
        xch = xc[:, sl]
        xb = xch.astype(BF16)
        r = _sigmoid(jnp.dot(xb, wa_ref[hd], preferred_element_type=F32) + ba_ref[:, sl])
        ig = _sigmoid(jnp.dot(xb, wi_ref[hd], preferred_element_type=F32) + bi_ref[:, sl])
        log_a = (-RG_C) * r * sp[:, sl]
        a = jnp.exp(log_a)
        a_s[:, sl] = a
        u_s[:, sl] = jnp.sqrt(-jnp.tanh(log_a) * (a * a + 1.0)) * (ig * xch)

    row = lax.broadcasted_iota(I32, (SUBLANES, rg), 0)

    def scan_group(g, hprev):
        r0 = pl.multiple_of(g * SUBLANES, SUBLANES)
        aa = a_s[pl.ds(r0, SUBLANES), :]
        bb = u_s[pl.ds(r0, SUBLANES), :]
        for s in (1, 2, 4):
            keep = row >= s
            bb = jnp.where(keep, aa * pltpu.roll(bb, s, 0) + bb, bb)
            aa = jnp.where(keep, aa * pltpu.roll(aa, s, 0), aa)
        h8 = aa * hprev + bb
        hs_s[pl.ds(r0, SUBLANES), :] = h8
        return jnp.broadcast_to(h8[SUBLANES - 1:SUBLANES, :], (SUBLANES, rg))

    hc_s[...] = lax.fori_loop(0, tt // SUBLANES, scan_group, hc_s[...])
    y_s[:, 0:rg] = _gelu(p_s[:, c1:c2]) * hs_s[...]

    vn = _rms(_gelu(p_s[:, c3:c4]), gvn_ref[...]).astype(BF16)
    tri = (lax.broadcasted_iota(I32, (chunk, chunk), 0) >= lax.broadcasted_iota(I32, (chunk, chunk), 1))
    for g in range(gm // HEAD_DIM):
        wg = jnp.where(tri, wsp_ref[g], 0.0).astype(BF16)
        bcol = bst_ref[:, g:g + 1]
        for c in range(tt // chunk):
            rs = slice(c * chunk, (c + 1) * chunk)
            cs = slice(g * HEAD_DIM, (g + 1) * HEAD_DIM)
            sv = jnp.dot(wg, vn[rs, cs], preferred_element_type=F32) + bcol
            y_s[rs, rg + g * HEAD_DIM:rg + (g + 1) * HEAD_DIM] = (
                _gelu(p_s[rs, c2 + g * HEAD_DIM:c2 + (g + 1) * HEAD_DIM]) * sv)

    scale = HEAD_DIM ** -0.5
    for hd in range(xa // HEAD_DIM):
        sl = slice(hd * HEAD_DIM, (hd + 1) * HEAD_DIM)
        q = p_s[:, c4 + hd * HEAD_DIM:c4 + (hd + 1) * HEAD_DIM].astype(BF16)
        s = lax.dot_general(q, k_ref[0, :, sl], (((1,), (1,)), ((), ())),
                            preferred_element_type=F32) * scale
        e = jnp.exp(s - jnp.max(s, axis=-1, keepdims=True))
        o = jnp.dot(e.astype(BF16), v_ref[0, :, sl], preferred_element_type=F32)
        y_s[:, rg + gm + hd * HEAD_DIM:rg + gm + (hd + 1) * HEAD_DIM] = o / jnp.sum(e, axis=-1, keepdims=True)

    for lo, hi in ((0, rg), (rg, rg + gm), (rg + gm, rg + gm + xa)):
        yb_s[:, lo:hi] = _rms(y_s[:, lo:hi], outg_ref[:, lo:hi]).astype(BF16)
    h = x + jnp.dot(yb_s[...], wout_ref[...], preferred_element_type=F32)
    h_ref[0] = h

    hn = _rms(h, lnffn_ref[...])
    _store_packed_rows(hnp_ref, hn, d // (2 * LANES))

    lg = lax.dot_general(wrt_ref[...], hn, (((1,), (1,)), ((), ())),
                         precision=lax.Precision.HIGHEST, preferred_element_type=F32) + br_ref[...]
    eid = lax.broadcasted_iota(I32, (n_exp, tt), 0)
    vals, sels = [], []
    for k in range(TOP_K):
        m = jnp.max(lg, axis=0, keepdims=True)
        ik = jnp.min(jnp.where(lg == m, eid, n_exp), axis=0, keepdims=True)
        sel = eid == ik
        vals.append(m)
        sels.append(sel)
        idx_ref[k:k + 1, :] = ik
        lg = jnp.where(sel, -jnp.inf, lg)
    es = [jnp.exp(v - vals[0]) for v in vals]
    den = es[0] + es[1] + es[2] + es[3]
    for k in range(TOP_K):
        gate_ref[k:k + 1, :] = es[k] / den
    oh = jnp.where(sels[0] | sels[1] | sels[2] | sels[3], 1.0, 0.0)
    upper = jnp.where(lax.broadcasted_iota(I32, (tt, tt), 0) < lax.broadcasted_iota(I32, (tt, tt), 1),
                      1.0, 0.0).astype(BF16)
    before = jnp.dot(oh.astype(BF16), upper, preferred_element_type=F32) + run_s[:, 0:1]
    for k in range(TOP_K):
        rank_ref[k:k + 1, :] = jnp.sum(jnp.where(sels[k], before, 0.0), axis=0, keepdims=True).astype(I32)
    run_s[...] = run_s[...] + jnp.sum(oh, axis=1, keepdims=True)
    cnt_ref[...] = run_s[...]


def _mixer(x, k, v, prm, *, tt):
    b, t, d = x.shape
    n_t = t // tt
    n = b * t
    rg = prm["conv_w"].shape[1]
    gm = prm["gvn"].shape[1]
    m = k.shape[1]
    xa = k.shape[2]
    chunk = prm["wsp"].shape[1]
    in_cols = prm["w_in"].shape[1]
    n_exp = prm["wrt"].shape[0]
    s_rows = d // (2 * LANES)
    const_names = ("lnmix", "w_in", "conv_w", "conv_b", "wa", "ba", "wi", "bi", "lam", "gvn", "wsp", "bst")
    const_names2 = ("outg", "w_out", "lnffn", "wrt", "br")
    in_specs = ([pl.BlockSpec((1, tt, d), lambda bi, ti: (bi, ti, 0))]
                + [_const_spec(prm[k].shape) for k in const_names]
                + [pl.BlockSpec((1, m, xa), lambda bi, ti: (bi, 0, 0)),
                   pl.BlockSpec((1, m, xa), lambda bi, ti: (bi, 0, 0))]
                + [_const_spec(prm[k].shape) for k in const_names2])
    tok = lambda bi, ti: (0, bi * n_t + ti)
    out_specs = [
        pl.BlockSpec((1, tt, d), lambda bi, ti: (bi, ti, 0)),
        pl.BlockSpec((tt * s_rows, LANES), lambda bi, ti: (bi * n_t + ti, 0)),
        pl.BlockSpec((TOP_K, tt), tok),
        pl.BlockSpec((TOP_K, tt), tok),
        pl.BlockSpec((TOP_K, tt), tok),
        pl.BlockSpec((n_exp, LANES), lambda bi, ti: (0, 0)),
    ]
    out_shape = [
        jax.ShapeDtypeStruct((b, t, d), F32),
        jax.ShapeDtypeStruct((n * s_rows, LANES), U32),
        jax.ShapeDtypeStruct((TOP_K, n), I32),
        jax.ShapeDtypeStruct((TOP_K, n), F32),
        jax.ShapeDtypeStruct((TOP_K, n), I32),
        jax.ShapeDtypeStruct((n_exp, LANES), F32),
    ]
    scratch = [
        pltpu.VMEM((tt, in_cols), F32),
        pltpu.VMEM((tt + 2 * SUBLANES, rg), F32),
        pltpu.VMEM((tt, rg), F32),
        pltpu.VMEM((tt, rg), F32),
        pltpu.VMEM((tt, rg), F32),
        pltpu.VMEM((SUBLANES, rg), F32),
        pltpu.VMEM((tt, rg + gm + xa), F32),
        pltpu.VMEM((tt, rg + gm + xa), BF16),
        pltpu.VMEM((n_exp, LANES), F32),
    ]
    return pl.pallas_call(
        functools.partial(_mixer_kernel, rg=rg, gm=gm, xa=xa, chunk=chunk),
        grid=(b, n_t),
        in_specs=in_specs,
        out_specs=out_specs,
        out_shape=out_shape,
        scratch_shapes=scratch,
        compiler_params=pltpu.CompilerParams(
            dimension_semantics=("arbitrary", "arbitrary"), vmem_limit_bytes=VMEM_LIMIT_BYTES),
        name="mixer",
    )(x, *[prm[c] for c in const_names], k, v, *[prm[c] for c in const_names2])


def _dispatch_kernel(zs_ref, zl_ref, na_ref, dest_ref, hnp_ref, xs_ref, zb, sem, zsem, *, tile, lag, tm, s_rows):
    step = pl.program_id(0)
    n_exp = zs_ref.shape[0]
    nb = xs_ref.shape[0] // (tm * s_rows)

    def for_each_zero_copy(act):
        def per_expert(e, carry):
            off = zs_ref[e]
            length = zl_ref[e]
            p = tm // 2
            while p >= 1:
                bit = length & p

                @pl.when(bit != 0)
                def _(off=off, p=p):
                    dst = xs_ref.at[pl.ds(pl.multiple_of(off * s_rows, s_rows), p * s_rows)]
                    act(pltpu.make_async_copy(zb.at[pl.ds(0, p * s_rows)], dst, zsem))
                off = off + bit
                p //= 2
            return carry
        lax.fori_loop(0, n_exp, per_expert, 0)

        def per_block(b, carry):
            dst = xs_ref.at[pl.ds(pl.multiple_of(b * (tm * s_rows), tm * s_rows), tm * s_rows)]
            act(pltpu.make_async_copy(zb, dst, zsem))
            return carry
        lax.fori_loop(na_ref[0], nb, per_block, 0)

    @pl.when(step == 0)
    def _():
        zb[...] = jnp.zeros(zb.shape, U32)
        for_each_zero_copy(lambda c: c.start())

    def row_copy(n, dst_row):
        src = hnp_ref.at[pl.ds(pl.multiple_of(n * s_rows, s_rows), s_rows)]
        return pltpu.make_async_copy(src, xs_ref.at[pl.ds(pl.multiple_of(dst_row, s_rows), s_rows)], sem)

    def wait_token():
        for _ in range(TOP_K):
            row_copy(0, 0).wait()

    def body(n, carry):
        for k in range(TOP_K):
            row_copy(n, dest_ref[k * tile + n]).start()

        @pl.when(n >= lag)
        def _():
            wait_token()
        return carry

    lax.fori_loop(0, tile, body, 0)

    def drain(_, carry):
        wait_token()
        return carry

    lax.fori_loop(0, lag, drain, 0)

    @pl.when(step == 0)
    def _():
        for_each_zero_copy(lambda c: c.wait())


def _dispatch(zero_start, zero_len, n_active, dest_rows_tiles, hnp, rows, *, tile, tm, s_rows):
    n = hnp.shape[0] // s_rows
    lag = min(DISPATCH_LAG, tile)
    grid_spec = pltpu.PrefetchScalarGridSpec(
        num_scalar_prefetch=3,
        grid=(n // tile,),
        in_specs=[
            pl.BlockSpec((TOP_K * tile,), lambda i, *_: (i,), memory_space=pltpu.SMEM),
            pl.BlockSpec((tile * s_rows, LANES), lambda i, *_: (i, 0)),
        ],
        out_specs=pl.BlockSpec(memory_space=pl.ANY),
        scratch_shapes=[pltpu.VMEM((tm * s_rows, LANES), U32), pltpu.SemaphoreType.DMA(()),
                        pltpu.SemaphoreType.DMA(())],
    )
    return pl.pallas_call(
        functools.partial(_dispatch_kernel, tile=tile, lag=lag, tm=tm, s_rows=s_rows),
        grid_spec=grid_spec,
        out_shape=jax.ShapeDtypeStruct((rows * s_rows, LANES), U32),
        compiler_params=pltpu.CompilerParams(
            dimension_semantics=("arbitrary",), vmem_limit_bytes=VMEM_LIMIT_BYTES,
            disable_bounds_checks=True, has_side_effects=True),
        name="dispatch",
    )(zero_start, zero_len, n_active, dest_rows_tiles, hnp)


def _expert_changed(be_ref, i):
    prev = be_ref[jnp.maximum(i - 1, 0)]
    return (i == 0) | (be_ref[i] != prev)


def _moe_up_kernel(be_ref, na_ref, xs_ref, w_ref, bg_ref, bu_ref, perm_ref, act_ref, wp_s, xb_s):
    i = pl.program_id(1)
    tn = w_ref.shape[2]
    hn_ = tn // 2
    d = xb_s.shape[1]
    active = i < na_ref[0]

    @pl.when(active & _expert_changed(be_ref, i))
    def _():
        for g in range(tn // MXU_DIM):
            wg = w_ref[0, :, g * MXU_DIM:(g + 1) * MXU_DIM].astype(BF16)
            wq = jnp.dot(wg, perm_ref[...], preferred_element_type=F32).astype(BF16)
            hw = MXU_DIM // 2
            wp_s[:, g * hw:(g + 1) * hw] = wq[:, 0:hw]
            wp_s[:, hn_ + g * hw:hn_ + (g + 1) * hw] = wq[:, hw:MXU_DIM]

    @pl.when(active)
    def _():
        half = d // 2
        s_rows = half // LANES
        tm = xb_s.shape[0]
        for c in range(s_rows):
            lo, hi = _load_packed_rows(xs_ref, c, tm, s_rows)
            xb_s[:, c * LANES:(c + 1) * LANES] = lo.astype(BF16)
            xb_s[:, half + c * LANES:half + (c + 1) * LANES] = hi.astype(BF16)
        gu = jnp.dot(xb_s[...], wp_s[...], preferred_element_type=F32)
        gate = jnp.minimum(gu[:, 0:hn_] + bg_ref[0], SWIGLU_LIMIT)
        up = jnp.clip(gu[:, hn_:tn] + bu_ref[0], -SWIGLU_LIMIT, SWIGLU_LIMIT)
        act_ref[...] = ((up + 1.0) * (gate * _sigmoid(SWIGLU_ALPHA * gate))).astype(BF16)

    @pl.when(jnp.logical_not(active))
    def _():
        act_ref[...] = jnp.zeros(act_ref.shape, BF16)


def _moe_up(block_expert, n_active, xs, w_gate_up, bg, bu, perm, *, tm, tn):
    n_exp, d, f2 = w_gate_up.shape
    s_rows = d // (2 * LANES)
    rows = xs.shape[0] // s_rows
    nb = rows // tm

    def blk(i, na):
        return jnp.minimum(i, na[0] - 1)

    grid_spec = pltpu.PrefetchScalarGridSpec(
        num_scalar_prefetch=2,
        grid=(f2 // tn, nb),
        in_specs=[
            pl.BlockSpec((tm * s_rows, LANES), lambda j, i, be, na: (blk(i, na), 0)),
            pl.BlockSpec((1, d, tn), lambda j, i, be, na: (be[blk(i, na)], 0, j)),
            pl.BlockSpec((1, 1, tn // 2), lambda j, i, be, na: (be[blk(i, na)], 0, j)),
            pl.BlockSpec((1, 1, tn // 2), lambda j, i, be, na: (be[blk(i, na)], 0, j)),
            pl.BlockSpec((MXU_DIM, MXU_DIM), lambda j, i, be, na: (0, 0)),
        ],
        out_specs=pl.BlockSpec((tm, tn // 2), lambda j, i, be, na: (i, j)),
        scratch_shapes=[pltpu.VMEM((d, tn), BF16), pltpu.VMEM((tm, d), BF16)],
    )
    return pl.pallas_call(
        _moe_up_kernel,
        grid_spec=grid_spec,
        out_shape=jax.ShapeDtypeStruct((rows, f2 // 2), BF16),
        compiler_params=pltpu.CompilerParams(
            dimension_semantics=("arbitrary", "arbitrary"), vmem_limit_bytes=VMEM_LIMIT_BYTES),
        name="moe_up",
    )(block_expert, n_active, xs, w_gate_up, bg, bu, perm)


def _moe_down_kernel(be_ref, na_ref, act_ref, w_ref, b_ref, ys_ref, wb_s, *, sub_shift):
    i = pl.program_id(0)
    blk = lax.shift_right_logical(i, sub_shift)
    active = blk < na_ref[0]
    first_sub = (i & ((1 << sub_shift) - 1)) == 0

    @pl.when(active & first_sub & _expert_changed(be_ref, blk))
    def _():
        wb_s[...] = w_ref[0].astype(BF16)

    @pl.when(active)
    def _():
        y = jnp.dot(act_ref[...], wb_s[...], preferred_element_type=F32) + b_ref[0]
        _store_packed_rows(ys_ref, y, y.shape[1] // (2 * LANES))

    @pl.when(jnp.logical_not(active))
    def _():
        ys_ref[...] = jnp.zeros(ys_ref.shape, U32)


def _moe_down(block_expert, n_active, act, w_down, b_down, *, tm, sub_rows):
    rows, f = act.shape
    n_exp, _, d = w_down.shape
    s_rows = d // (2 * LANES)
    sub = tm // sub_rows
    sub_shift = sub.bit_length() - 1
    assert (1 << sub_shift) == sub and sub * sub_rows == tm
    n_steps = rows // sub_rows

    def blk(i, na):
        return jnp.minimum(lax.shift_right_logical(i, sub_shift), na[0] - 1)

    grid_spec = pltpu.PrefetchScalarGridSpec(
        num_scalar_prefetch=2,
        grid=(n_steps,),
        in_specs=[
            pl.BlockSpec((sub_rows, f), lambda i, be, na: (jnp.minimum(i, na[0] * sub - 1), 0)),
            pl.BlockSpec((1, f, d), lambda i, be, na: (be[blk(i, na)], 0, 0)),
            pl.BlockSpec((1, 1, d), lambda i, be, na: (be[blk(i, na)], 0, 0)),
        ],
        out_specs=pl.BlockSpec((sub_rows * s_rows, LANES), lambda i, be, na: (i, 0)),
        scratch_shapes=[pltpu.VMEM((f, d), BF16)],
    )
    return pl.pallas_call(
        functools.partial(_moe_down_kernel, sub_shift=sub_shift),
        grid_spec=grid_spec,
        out_shape=jax.ShapeDtypeStruct((rows * s_rows, LANES), U32),
        compiler_params=pltpu.CompilerParams(
            dimension_semantics=("arbitrary",), vmem_limit_bytes=VMEM_LIMIT_BYTES),
        name="moe_down",
    )(block_expert, n_active, act, w_down, b_down)


def _combine_kernel(dcur_ref, dnext_ref, ys_ref, h_ref, gates_ref, fg_ref, out_ref, buf, sem, *, tile, s_rows):
    s = pl.program_id(0)
    n_steps = pl.num_programs(0)

    def row_copy(slot, k, n, src_row):
        src = ys_ref.at[pl.ds(pl.multiple_of(src_row, s_rows), s_rows)]
        dst = buf.at[slot, k, pl.ds(pl.multiple_of(n * s_rows, s_rows), s_rows)]
        return pltpu.make_async_copy(src, dst, sem.at[slot])

    def issue(d_ref, slot):
        def body(n, carry):
            for k in range(TOP_K):
                row_copy(slot, k, n, d_ref[k * tile + n]).start()
            return carry
        lax.fori_loop(0, tile, body, 0)

    @pl.when(s == 0)
    def _():
        issue(dcur_ref, 0)

    @pl.when(s + 1 < n_steps)
    def _():
        issue(dnext_ref, (s + 1) % 2)

    def consume(slot):
        def wait_body(n, carry):
            for k in range(TOP_K):
                row_copy(slot, k, 0, 0).wait()
            return carry
        lax.fori_loop(0, tile, wait_body, 0)
        gs = [gates_ref[:, k:k + 1] for k in range(TOP_K)]
        d = out_ref.shape[1]
        half = d // 2
        ssq = jnp.zeros((tile, 1), F32)
        for c in range(s_rows):
            lo_cols = slice(c * LANES, (c + 1) * LANES)
            hi_cols = slice(half + c * LANES, half + (c + 1) * LANES)
            acc_lo = h_ref[:, lo_cols]
            acc_hi = h_ref[:, hi_cols]
            for k in range(TOP_K):
                lo, hi = _load_packed_rows(buf.at[slot, k], c, tile, s_rows)
                acc_lo = acc_lo + gs[k] * lo
                acc_hi = acc_hi + gs[k] * hi
            out_ref[:, lo_cols] = acc_lo
            out_ref[:, hi_cols] = acc_hi
            ssq = ssq + jnp.sum(acc_lo * acc_lo + acc_hi * acc_hi, axis=-1, keepdims=True)
        out_ref[...] = out_ref[...] * lax.rsqrt(ssq / d + EPS) * fg_ref[...]

    for slot in range(2):
        @pl.when(s % 2 == slot)
        def _(slot=slot):
            consume(slot)


def _combine(src_rows_tiles, ys, h2, gates_t, fg, *, tile):
    n, d = h2.shape
    s_rows = d // (2 * LANES)
    n_steps = n // tile
    return pl.pallas_call(
        functools.partial(_combine_kernel, tile=tile, s_rows=s_rows),
        grid=(n_steps,),
        in_specs=[
            pl.BlockSpec((TOP_K * tile,), lambda i: (i,), memory_space=pltpu.SMEM),
            pl.BlockSpec((TOP_K * tile,), lambda i: (jnp.minimum(i + 1, n_steps - 1),), memory_space=pltpu.SMEM),
            pl.BlockSpec(memory_space=pl.ANY),
            pl.BlockSpec((tile, d), lambda i: (i, 0)),
            pl.BlockSpec((tile, TOP_K), lambda i: (i, 0)),
            _const_spec((1, d)),
        ],
        out_specs=pl.BlockSpec((tile, d), lambda i: (i, 0)),
        out_shape=jax.ShapeDtypeStruct((n, d), F32),
        scratch_shapes=[pltpu.VMEM((2, TOP_K, tile * s_rows, LANES), U32), pltpu.SemaphoreType.DMA((2,))],
        compiler_params=pltpu.CompilerParams(
            dimension_semantics=("arbitrary",), vmem_limit_bytes=VMEM_LIMIT_BYTES,
            disable_bounds_checks=True),
        name="combine",
    )(src_rows_tiles, src_rows_tiles, ys, h2, gates_t, fg)


def _tile_major(a, tile):
    k, n = a.shape
    return a.reshape(k, n // tile, tile).transpose(1, 0, 2).reshape(-1)


def _layer(h, mem, ln_mix_g, w_in, conv_w, conv_b, w_rg_a, b_rg_a, w_rg_i, b_rg_i, lru_lambda, gm_v_norm_g,
           w_spatial, b_spatial, mem_norm_g, w_mem_k, w_mem_v, out_norm_g, w_out, ln_ffn_g, w_router, b_router,
           w_gate_up, b_gate_up, w_down, b_down, final_g, *, mix_tile, moe_tile, up_tn, down_rows,
           dispatch_tile, combine_tile):
    b, t, d = h.shape
    n = b * t
    n_exp = w_router.shape[1]
    row = lambda a: a.reshape(1, -1)
    km, v = _mem_kv(mem, row(mem_norm_g), w_mem_k.astype(BF16), w_mem_v.astype(BF16))
    prm = dict(
        lnmix=row(ln_mix_g), w_in=w_in.astype(BF16), conv_w=conv_w, conv_b=row(conv_b),
        wa=w_rg_a.astype(BF16), ba=row(b_rg_a), wi=w_rg_i.astype(BF16), bi=row(b_rg_i), lam=row(lru_lambda),
        gvn=row(gm_v_norm_g), wsp=w_spatial, bst=b_spatial.T, outg=row(out_norm_g), w_out=w_out.astype(BF16),
        lnffn=row(ln_ffn_g), wrt=w_router.T, br=b_router.reshape(-1, 1))
    h1, hnp, idx, gates, rank, cnt = _mixer(h, km, v, prm, tt=mix_tile)

    counts = cnt[:, 0].astype(I32)
    padded = ((counts + moe_tile - 1) // moe_tile) * moe_tile
    pad_end = jnp.cumsum(padded)
    pad_start = pad_end - padded
    s_rows = d // (2 * LANES)
    experts = jnp.arange(n_exp, dtype=I32)
    start_of = jnp.sum(jnp.where(idx[..., None] == experts, pad_start, 0), axis=-1)
    dest_rows = (start_of + rank) * s_rows
    nb = (n * TOP_K) // moe_tile + n_exp
    n_active = (pad_end[-1] // moe_tile).astype(I32).reshape(1)
    block_first = jnp.arange(nb, dtype=I32) * moe_tile
    block_expert = jnp.minimum(
        jnp.sum((pad_end[None, :] <= block_first[:, None]).astype(I32), axis=-1), n_exp - 1)

    xs = _dispatch(pad_start + counts, padded - counts, n_active, _tile_major(dest_rows, dispatch_tile), hnp,
                   nb * moe_tile, tile=dispatch_tile, tm=moe_tile, s_rows=s_rows)
    f = w_down.shape[1]
    r_ = jnp.arange(MXU_DIM)
    src = jnp.where(r_ < MXU_DIM // 2, 2 * r_, 2 * (r_ - MXU_DIM // 2) + 1)
    perm = (jnp.arange(MXU_DIM)[:, None] == src[None, :]).astype(BF16)
    bg = b_gate_up[:, 0::2].reshape(n_exp, 1, f)
    bu = b_gate_up[:, 1::2].reshape(n_exp, 1, f)
    act = _moe_up(block_expert, n_active, xs, w_gate_up, bg, bu, perm, tm=moe_tile, tn=up_tn)
    ys = _moe_down(block_expert, n_active, act, w_down, b_down.reshape(n_exp, 1, d), tm=moe_tile,
                   sub_rows=down_rows)
    out = _combine(_tile_major(dest_rows, combine_tile), ys, h1.reshape(n, d), gates.T, row(final_g),
                   tile=combine_tile)
    return out.reshape(b, t, d)


def kernel(x, mem, ln_mix_g, w_in, conv_w, conv_b, w_rg_a, b_rg_a, w_rg_i, b_rg_i, lru_lambda, gm_v_norm_g, w_spatial, b_spatial, mem_norm_g, w_mem_k, w_mem_v, out_norm_g, w_out, ln_ffn_g, w_router, b_router, w_gate_up, b_gate_up, w_down, b_down, final_norm_g):
    depth = w_in.shape[0]
    assert depth == 1, "the final RMSNorm is fused into the single layer's combine stage"
    return _layer(
        x, mem, ln_mix_g[0], w_in[0], conv_w[0], conv_b[0], w_rg_a[0], b_rg_a[0], w_rg_i[0], b_rg_i[0],
        lru_lambda[0], gm_v_norm_g[0], w_spatial[0], b_spatial[0], mem_norm_g[0], w_mem_k[0], w_mem_v[0],
        out_norm_g[0], w_out[0], ln_ffn_g[0], w_router[0], b_router[0], w_gate_up[0], b_gate_up[0], w_down[0],
        b_down[0], final_norm_g,
        mix_tile=MIX_TILE, moe_tile=MOE_TILE, up_tn=UP_TN, down_rows=DOWN_ROWS,
        dispatch_tile=DISPATCH_TILE, combine_tile=COMBINE_TILE)
```

```python
import functools

import jax
import jax.numpy as jnp
from jax import lax
from jax.experimental import pallas as pl
from jax.experimental.pallas import tpu as pltpu

F32 = jnp.float32
BF16 = jnp.bfloat16
I32 = jnp.int32
U32 = jnp.uint32

HEAD_DIM = 128
RG_C = 8.0
TOP_K = 4
SWIGLU_LIMIT = 7.0
SWIGLU_ALPHA = 1.702
EPS = 1e-6

LANES = 128
SUBLANES = 8
MXU_DIM = 256
VMEM_LIMIT_BYTES = 56 * 1024 * 1024

MIX_TILE = 256
MOE_TILE = 512
UP_TN = 2048
DOWN_ROWS = 512
ROW_CHUNK = 128
DISPATCH_TILE = 512
DISPATCH_LAG = 64
COMBINE_TILE = 256


def _rms(x, g):
    return x * lax.rsqrt(jnp.mean(x * x, axis=-1, keepdims=True) + EPS) * g


def _gelu(x):
    return x * (0.5 * (1.0 + jnp.tanh(0.7978845608028654 * (x + 0.044715 * (x * x * x)))))


def _sigmoid(x):
    return 1.0 / (1.0 + jnp.exp(-x))


def _const_spec(shape):
    nd = len(shape)
    return pl.BlockSpec(shape, lambda *_: (0,) * nd, pipeline_mode=pl.Buffered(1))


def _store_packed_rows(ref, y, s_rows, row0=0):
    rows, d = y.shape
    half = d // 2
    for c in range(s_rows):
        lo = y[:, c * LANES:(c + 1) * LANES].astype(BF16).astype(F32)
        hi = y[:, half + c * LANES:half + (c + 1) * LANES].astype(BF16).astype(F32)
        word = (pltpu.bitcast(lo, U32) >> 16) | (pltpu.bitcast(hi, U32) & jnp.uint32(0xFFFF0000))
        ref[pl.ds(row0 * s_rows + c, rows, stride=s_rows), :] = word


def _load_packed_rows(ref, c, rows, s_rows, row0=0):
    word = ref[pl.ds(row0 * s_rows + c, rows, stride=s_rows), :]
    return pltpu.bitcast(word << 16, F32), pltpu.bitcast(word & jnp.uint32(0xFFFF0000), F32)


def _mem_kv_kernel(mem_ref, g_ref, wk_ref, wv_ref, k_ref, v_ref):
    mn = _rms(mem_ref[0], g_ref[...]).astype(BF16)
    k_ref[0] = jnp.dot(mn, wk_ref[...], preferred_element_type=F32).astype(BF16)
    v_ref[0] = jnp.dot(mn, wv_ref[...], preferred_element_type=F32).astype(BF16)


def _mem_kv(mem, g, wk, wv):
    b, m, d = mem.shape
    xa = wk.shape[1]
    return pl.pallas_call(
        _mem_kv_kernel,
        grid=(b,),
        in_specs=[
            pl.BlockSpec((1, m, d), lambda i: (i, 0, 0)),
            _const_spec((1, d)),
            _const_spec((d, xa)),
            _const_spec((d, xa)),
        ],
        out_specs=[
            pl.BlockSpec((1, m, xa), lambda i: (i, 0, 0)),
            pl.BlockSpec((1, m, xa), lambda i: (i, 0, 0)),
        ],
        out_shape=[jax.ShapeDtypeStruct((b, m, xa), BF16), jax.ShapeDtypeStruct((b, m, xa), BF16)],
        compiler_params=pltpu.CompilerParams(
            dimension_semantics=("arbitrary",), vmem_limit_bytes=VMEM_LIMIT_BYTES),
        name="mem_kv",
    )(mem, g, wk, wv)


def _mixer_kernel(x_ref, lnmix_ref, win_ref, cw_ref, cb_ref, wa_ref, ba_ref, wi_ref, bi_ref, lam_ref,
                  gvn_ref, wsp_ref, bst_ref, k_ref, v_ref, outg_ref, wout_ref, lnffn_ref, wrt_ref, br_ref,
                  h_ref, hnp_ref, idx_ref, gate_ref, rank_ref, cnt_ref,
                  p_s, xr_s, a_s, u_s, hs_s, hc_s, y_s, yb_s, run_s,
                  *, rg, gm, xa, chunk):
    tt = x_ref.shape[1]
    d = x_ref.shape[2]
    n_exp = wrt_ref.shape[0]
    c1, c2, c3, c4 = rg, 2 * rg, 2 * rg + gm, 2 * rg + 2 * gm
    bi_ = pl.program_id(0)
    ti_ = pl.program_id(1)

    @pl.when(ti_ == 0)
    def _():
        xr_s[0:SUBLANES, :] = jnp.zeros((SUBLANES, rg), F32)
        hc_s[...] = jnp.zeros((SUBLANES, rg), F32)

    @pl.when((ti_ == 0) & (bi_ == 0))
    def _():
        run_s[...] = jnp.zeros(run_s.shape, F32)

    x = x_ref[0]
    xn = _rms(x, lnmix_ref[...]).astype(BF16)
    p_s[...] = jnp.dot(xn, win_ref[...], preferred_element_type=F32)

    xr_s[SUBLANES:SUBLANES + tt, :] = p_s[:, 0:c1]
    cw = cw_ref[...]
    kw = cw.shape[0]
    xc = cb_ref[...] + cw[0:1, :] * xr_s[pl.ds(SUBLANES - kw + 1, tt), :]
    for w in range(1, kw):
        xc = xc + cw[w:w + 1, :] * xr_s[pl.ds(SUBLANES - kw + 1 + w, tt), :]
    xr_s[0:SUBLANES, :] = xr_s[tt:tt + SUBLANES, :]

    z = -lam_ref[...]
    sp = jnp.maximum(z, 0.0) + jnp.log1p(jnp.exp(-jnp.abs(z)))
    for hd in range(rg // HEAD_DIM):
        sl = slice(hd * HEAD_DIM, (hd + 1) * HEAD_DIM)
        xch = xc[:, sl]
        xb = xch.astype(BF16)
        r = _sigmoid(jnp.dot(xb, wa_ref[hd], preferred_element_type=F32) + ba_ref[:, sl])
        ig = _sigmoid(jnp.dot(xb, wi_ref[hd], preferred_element_type=F32) + bi_ref[:, sl])
        log_a = (-RG_C) * r * sp[:, sl]
        a = jnp.exp(log_a)
        a_s[:, sl] = a
        u_s[:, sl] = jnp.sqrt(-jnp.tanh(log_a) * (a * a + 1.0)) * (ig * xch)

    row = lax.broadcasted_iota(I32, (SUBLANES, rg), 0)

    def scan_group(g, hprev):
        r0 = pl.multiple_of(g * SUBLANES, SUBLANES)
        aa = a_s[pl.ds(r0, SUBLANES), :]
        bb = u_s[pl.ds(r0, SUBLANES), :]
        for s in (1, 2, 4):
            keep = row >= s
            bb = jnp.where(keep, aa * pltpu.roll(bb, s, 0) + bb, bb)
            aa = jnp.where(keep, aa * pltpu.roll(aa, s, 0), aa)
        h8 = aa * hprev + bb
        hs_s[pl.ds(r0, SUBLANES), :] = h8
        return jnp.broadcast_to(h8[SUBLANES - 1:SUBLANES, :], (SUBLANES, rg))

    hc_s[...] = lax.fori_loop(0, tt // SUBLANES, scan_group, hc_s[...])
    y_s[:, 0:rg] = _gelu(p_s[:, c1:c2]) * hs_s[...]

    vn = _rms(_gelu(p_s[:, c3:c4]), gvn_ref[...]).astype(BF16)
    tri = (lax.broadcasted_iota(I32, (chunk, chunk), 0) >= lax.broadcasted_iota(I32, (chunk, chunk), 1))
    for g in range(gm // HEAD_DIM):
        wg = jnp.where(tri, wsp_ref[g], 0.0).astype(BF16)
        bcol = bst_ref[:, g:g + 1]
        for c in range(tt // chunk):
            rs = slice(c * chunk, (c + 1) * chunk)
            cs = slice(g * HEAD_DIM, (g + 1) * HEAD_DIM)
            sv = jnp.dot(wg, vn[rs, cs], preferred_element_type=F32) + bcol
            y_s[rs, rg + g * HEAD_DIM:rg + (g + 1) * HEAD_DIM] = (
                _gelu(p_s[rs, c2 + g * HEAD_DIM:c2 + (g + 1) * HEAD_DIM]) * sv)

    scale = HEAD_DIM ** -0.5
    for hd in range(xa // HEAD_DIM):
        sl = slice(hd * HEAD_DIM, (hd + 1) * HEAD_DIM)
        q = p_s[:, c4 + hd * HEAD_DIM:c4 + (hd + 1) * HEAD_DIM].astype(BF16)
        s = lax.dot_general(q, k_ref[0, :, sl], (((1,), (1,)), ((), ())),
                            preferred_element_type=F32) * scale
        e = jnp.exp(s - jnp.max(s, axis=-1, keepdims=True))
        o = jnp.dot(e.astype(BF16), v_ref[0, :, sl], preferred_element_type=F32)
        y_s[:, rg + gm + hd * HEAD_DIM:rg + gm + (hd + 1) * HEAD_DIM] = o / jnp.sum(e, axis=-1, keepdims=True)

    for lo, hi in ((0, rg), (rg, rg + gm), (rg + gm, rg + gm + xa)):
        yb_s[:, lo:hi] = _rms(y_s[:, lo:hi], outg_ref[:, lo:hi]).astype(BF16)
    h = x + jnp.dot(yb_s[...], wout_ref[...], preferred_element_type=F32)
    h_ref[0] = h

    hn = _rms(h, lnffn_ref[...])
    _store_packed_rows(hnp_ref, hn, d // (2 * LANES))

    lg = lax.dot_general(wrt_ref[...], hn, (((1,), (1,)), ((), ())),
                         precision=lax.Precision.HIGHEST, preferred_element_type=F32) + br_ref[...]
    eid = lax.broadcasted_iota(I32, (n_exp, tt), 0)
    vals, sels = [], []
    for k in range(TOP_K):
        m = jnp.max(lg, axis=0, keepdims=True)
        ik = jnp.min(jnp.where(lg == m, eid, n_exp), axis=0, keepdims=True)
        sel = eid == ik
        vals.append(m)
        sels.append(sel)
        idx_ref[k:k + 1, :] = ik
        lg = jnp.where(sel, -jnp.inf, lg)
    es = [jnp.exp(v - vals[0]) for v in vals]
    den = es[0] + es[1] + es[2] + es[3]
    for k in range(TOP_K):
        gate_ref[k:k + 1, :] = es[k] / den
    oh = jnp.where(sels[0] | sels[1] | sels[2] | sels[3], 1.0, 0.0)
    upper = jnp.where(lax.broadcasted_iota(I32, (tt, tt), 0) < lax.broadcasted_iota(I32, (tt, tt), 1),
                      1.0, 0.0).astype(BF16)
    before = jnp.dot(oh.astype(BF16), upper, preferred_element_type=F32) + run_s[:, 0:1]
    for k in range(TOP_K):
        rank_ref[k:k + 1, :] = jnp.sum(jnp.where(sels[k], before, 0.0), axis=0, keepdims=True).astype(I32)
    run_s[...] = run_s[...] + jnp.sum(oh, axis=1, keepdims=True)
    cnt_ref[...] = run_s[...]


def _mixer(x, k, v, prm, *, tt):
    b, t, d = x.shape
    n_t = t // tt
    n = b * t
    rg = prm["conv_w"].shape[1]
    gm = prm["gvn"].shape[1]
    m = k.shape[1]
    xa = k.shape[2]
    chunk = prm["wsp"].shape[1]
    in_cols = prm["w_in"].shape[1]
    n_exp = prm["wrt"].shape[0]
    s_rows = d // (2 * LANES)
    const_names = ("lnmix", "w_in", "conv_w", "conv_b", "wa", "ba", "wi", "bi", "lam", "gvn", "wsp", "bst")
    const_names2 = ("outg", "w_out", "lnffn", "wrt", "br")
    in_specs = ([pl.BlockSpec((1, tt, d), lambda bi, ti: (bi, ti, 0))]
                + [_const_spec(prm[k].shape) for k in const_names]
                + [pl.BlockSpec((1, m, xa), lambda bi, ti: (bi, 0, 0)),
                   pl.BlockSpec((1, m, xa), lambda bi, ti: (bi, 0, 0))]
                + [_const_spec(prm[k].shape) for k in const_names2])
    tok = lambda bi, ti: (0, bi * n_t + ti)
    out_specs = [
        pl.BlockSpec((1, tt, d), lambda bi, ti: (bi, ti, 0)),
        pl.BlockSpec((tt * s_rows, LANES), lambda bi, ti: (bi * n_t + ti, 0)),
        pl.BlockSpec((TOP_K, tt), tok),
        pl.BlockSpec((TOP_K, tt), tok),
        pl.BlockSpec((TOP_K, tt), tok),
        pl.BlockSpec((n_exp, LANES), lambda bi, ti: (0, 0)),
    ]
    out_shape = [
        jax.ShapeDtypeStruct((b, t, d), F32),
        jax.ShapeDtypeStruct((n * s_rows, LANES), U32),
        jax.ShapeDtypeStruct((TOP_K, n), I32),
        jax.ShapeDtypeStruct((TOP_K, n), F32),
        jax.ShapeDtypeStruct((TOP_K, n), I32),
        jax.ShapeDtypeStruct((n_exp, LANES), F32),
    ]
    scratch = [
        pltpu.VMEM((tt, in_cols), F32),
        pltpu.VMEM((tt + 2 * SUBLANES, rg), F32),
        pltpu.VMEM((tt, rg), F32),
        pltpu.VMEM((tt, rg), F32),
        pltpu.VMEM((tt, rg), F32),
        pltpu.VMEM((SUBLANES, rg), F32),
        pltpu.VMEM((tt, rg + gm + xa), F32),
        pltpu.VMEM((tt, rg + gm + xa), BF16),
        pltpu.VMEM((n_exp, LANES), F32),
    ]
    return pl.pallas_call(
        functools.partial(_mixer_kernel, rg=rg, gm=gm, xa=xa, chunk=chunk),
        grid=(b, n_t),
        in_specs=in_specs,
        out_specs=out_specs,
        out_shape=out_shape,
        scratch_shapes=scratch,
        compiler_params=pltpu.CompilerParams(
            dimension_semantics=("arbitrary", "arbitrary"), vmem_limit_bytes=VMEM_LIMIT_BYTES),
        name="mixer",
    )(x, *[prm[c] for c in const_names], k, v, *[prm[c] for c in const_names2])


def _dispatch_kernel(zs_ref, zl_ref, na_ref, dest_ref, hnp_ref, xs_ref, zb, sem, zsem, *, tile, lag, tm, s_rows):
    step = pl.program_id(0)
    n_exp = zs_ref.shape[0]
    nb = xs_ref.shape[0] // (tm * s_rows)

    def for_each_zero_copy(act):
        def per_expert(e, carry):
            off = zs_ref[e]
            length = zl_ref[e]
            p = tm // 2
            while p >= 1:
                bit = length & p

                @pl.when(bit != 0)
                def _(off=off, p=p):
                    dst = xs_ref.at[pl.ds(pl.multiple_of(off * s_rows, s_rows), p * s_rows)]
                    act(pltpu.make_async_copy(zb.at[pl.ds(0, p * s_rows)], dst, zsem))
                off = off + bit
                p //= 2
            return carry
        lax.fori_loop(0, n_exp, per_expert, 0)

        def per_block(b, carry):
            dst = xs_ref.at[pl.ds(pl.multiple_of(b * (tm * s_rows), tm * s_rows), tm * s_rows)]
            act(pltpu.make_async_copy(zb, dst, zsem))
            return carry
        lax.fori_loop(na_ref[0], nb, per_block, 0)

    @pl.when(step == 0)
    def _():
        zb[...] = jnp.zeros(zb.shape, U32)
        for_each_zero_copy(lambda c: c.start())

    def row_copy(n, dst_row):
        src = hnp_ref.at[pl.ds(pl.multiple_of(n * s_rows, s_rows), s_rows)]
        return pltpu.make_async_copy(src, xs_ref.at[pl.ds(pl.multiple_of(dst_row, s_rows), s_rows)], sem)

    def wait_token():
        for _ in range(TOP_K):
            row_copy(0, 0).wait()

    def body(n, carry):
        for k in range(TOP_K):
            row_copy(n, dest_ref[k * tile + n]).start(priority=k % 2)

        @pl.when(n >= lag)
        def _():
            wait_token()
        return carry

    lax.fori_loop(0, tile, body, 0)

    def drain(_, carry):
        wait_token()
        return carry

    lax.fori_loop(0, lag, drain, 0)

    @pl.when(step == 0)
    def _():
        for_each_zero_copy(lambda c: c.wait())


def _dispatch(zero_start, zero_len, n_active, dest_rows_tiles, hnp, rows, *, tile, tm, s_rows):
    n = hnp.shape[0] // s_rows
    lag = min(DISPATCH_LAG, tile)
    grid_spec = pltpu.PrefetchScalarGridSpec(
        num_scalar_prefetch=3,
        grid=(n // tile,),
        in_specs=[
            pl.BlockSpec((TOP_K * tile,), lambda i, *_: (i,), memory_space=pltpu.SMEM),
            pl.BlockSpec((tile * s_rows, LANES), lambda i, *_: (i, 0)),
        ],
        out_specs=pl.BlockSpec(memory_space=pl.ANY),
        scratch_shapes=[pltpu.VMEM((tm * s_rows, LANES), U32), pltpu.SemaphoreType.DMA(()),
                        pltpu.SemaphoreType.DMA(())],
    )
    return pl.pallas_call(
        functools.partial(_dispatch_kernel, tile=tile, lag=lag, tm=tm, s_rows=s_rows),
        grid_spec=grid_spec,
        out_shape=jax.ShapeDtypeStruct((rows * s_rows, LANES), U32),
        compiler_params=pltpu.CompilerParams(
            dimension_semantics=("arbitrary",), vmem_limit_bytes=VMEM_LIMIT_BYTES,
            disable_bounds_checks=True, has_side_effects=True),
        name="dispatch",
    )(zero_start, zero_len, n_active, dest_rows_tiles, hnp)


def _expert_changed(be_ref, i):
    prev = be_ref[jnp.maximum(i - 1, 0)]
    return (i == 0) | (be_ref[i] != prev)


def _moe_up_kernel(be_ref, na_ref, xs_ref, w_ref, bg_ref, bu_ref, perm_ref, act_ref, wp_s, xb_s):
    i = pl.program_id(1)
    tn = w_ref.shape[2]
    hn_ = tn // 2
    d = xb_s.shape[1]
    active = i < na_ref[0]

    @pl.when(active & _expert_changed(be_ref, i))
    def _():
        for g in range(tn // MXU_DIM):
            wg = w_ref[0, :, g * MXU_DIM:(g + 1) * MXU_DIM].astype(BF16)
            wq = jnp.dot(wg, perm_ref[...], preferred_element_type=F32).astype(BF16)
            hw = MXU_DIM // 2
            wp_s[:, g * hw:(g + 1) * hw] = wq[:, 0:hw]
            wp_s[:, hn_ + g * hw:hn_ + (g + 1) * hw] = wq[:, hw:MXU_DIM]

    @pl.when(active)
    def _():
        half = d // 2
        s_rows = half // LANES
        tm = xb_s.shape[0]
        rc = min(ROW_CHUNK, tm)
        for r0 in range(0, tm, rc):
            rows = slice(r0, r0 + rc)
            for c in range(s_rows):
                lo, hi = _load_packed_rows(xs_ref, c, rc, s_rows, row0=r0)
                xb_s[rows, c * LANES:(c + 1) * LANES] = lo.astype(BF16)
                xb_s[rows, half + c * LANES:half + (c + 1) * LANES] = hi.astype(BF16)
            gu = jnp.dot(xb_s[rows, :], wp_s[...], preferred_element_type=F32)
            gate = jnp.minimum(gu[:, 0:hn_] + bg_ref[0], SWIGLU_LIMIT)
            up = jnp.clip(gu[:, hn_:tn] + bu_ref[0], -SWIGLU_LIMIT, SWIGLU_LIMIT)
            act_ref[rows, :] = ((up + 1.0) * (gate * _sigmoid(SWIGLU_ALPHA * gate))).astype(BF16)

    @pl.when(jnp.logical_not(active))
    def _():
        act_ref[...] = jnp.zeros(act_ref.shape, BF16)


def _moe_up(block_expert, n_active, xs, w_gate_up, bg, bu, perm, *, tm, tn):
    n_exp, d, f2 = w_gate_up.shape
    s_rows = d // (2 * LANES)
    rows = xs.shape[0] // s_rows
    nb = rows // tm

    def blk(i, na):
        return jnp.minimum(i, na[0] - 1)

    grid_spec = pltpu.PrefetchScalarGridSpec(
        num_scalar_prefetch=2,
        grid=(f2 // tn, nb),
        in_specs=[
            pl.BlockSpec((tm * s_rows, LANES), lambda j, i, be, na: (blk(i, na), 0)),
            pl.BlockSpec((1, d, tn), lambda j, i, be, na: (be[blk(i, na)], 0, j)),
            pl.BlockSpec((1, 1, tn // 2), lambda j, i, be, na: (be[blk(i, na)], 0, j)),
            pl.BlockSpec((1, 1, tn // 2), lambda j, i, be, na: (be[blk(i, na)], 0, j)),
            pl.BlockSpec((MXU_DIM, MXU_DIM), lambda j, i, be, na: (0, 0)),
        ],
        out_specs=pl.BlockSpec((tm, tn // 2), lambda j, i, be, na: (i, j)),
        scratch_shapes=[pltpu.VMEM((d, tn), BF16), pltpu.VMEM((tm, d), BF16)],
    )
    return pl.pallas_call(
        _moe_up_kernel,
        grid_spec=grid_spec,
        out_shape=jax.ShapeDtypeStruct((rows, f2 // 2), BF16),
        compiler_params=pltpu.CompilerParams(
            dimension_semantics=("arbitrary", "arbitrary"), vmem_limit_bytes=VMEM_LIMIT_BYTES),
        name="moe_up",
    )(block_expert, n_active, xs, w_gate_up, bg, bu, perm)


def _moe_down_kernel(be_ref, na_ref, act_ref, w_ref, b_ref, ys_ref, wb_s, *, sub_shift):
    i = pl.program_id(0)
    blk = lax.shift_right_logical(i, sub_shift)
    active = blk < na_ref[0]
    first_sub = (i & ((1 << sub_shift) - 1)) == 0

    @pl.when(active & first_sub & _expert_changed(be_ref, blk))
    def _():
        wb_s[...] = w_ref[0].astype(BF16)

    @pl.when(active)
    def _():
        n_rows = act_ref.shape[0]
        rc = min(ROW_CHUNK, n_rows)
        for r0 in range(0, n_rows, rc):
            y = jnp.dot(act_ref[r0:r0 + rc, :], wb_s[...], preferred_element_type=F32) + b_ref[0]
            _store_packed_rows(ys_ref, y, y.shape[1] // (2 * LANES), row0=r0)

    @pl.when(jnp.logical_not(active))
    def _():
        ys_ref[...] = jnp.zeros(ys_ref.shape, U32)


def _moe_down(block_expert, n_active, act, w_down, b_down, *, tm, sub_rows):
    rows, f = act.shape
    n_exp, _, d = w_down.shape
    s_rows = d // (2 * LANES)
    sub = tm // sub_rows
    sub_shift = sub.bit_length() - 1
    assert (1 << sub_shift) == sub and sub * sub_rows == tm
    n_steps = rows // sub_rows

    def blk(i, na):
        return jnp.minimum(lax.shift_right_logical(i, sub_shift), na[0] - 1)

    grid_spec = pltpu.PrefetchScalarGridSpec(
        num_scalar_prefetch=2,
        grid=(n_steps,),
        in_specs=[
            pl.BlockSpec((sub_rows, f), lambda i, be, na: (jnp.minimum(i, na[0] * sub - 1), 0)),
            pl.BlockSpec((1, f, d), lambda i, be, na: (be[blk(i, na)], 0, 0)),
            pl.BlockSpec((1, 1, d), lambda i, be, na: (be[blk(i, na)], 0, 0)),
        ],
        out_specs=pl.BlockSpec((sub_rows * s_rows, LANES), lambda i, be, na: (i, 0)),
        scratch_shapes=[pltpu.VMEM((f, d), BF16)],
    )
    return pl.pallas_call(
        functools.partial(_moe_down_kernel, sub_shift=sub_shift),
        grid_spec=grid_spec,
        out_shape=jax.ShapeDtypeStruct((rows * s_rows, LANES), U32),
        compiler_params=pltpu.CompilerParams(
            dimension_semantics=("arbitrary",), vmem_limit_bytes=VMEM_LIMIT_BYTES),
        name="moe_down",
    )(block_expert, n_active, act, w_down, b_down)


def _combine_kernel(dcur_ref, dnext_ref, ys_ref, h_ref, gates_ref, fg_ref, out_ref, buf, sem, *, tile, s_rows):
    s = pl.program_id(0)
    n_steps = pl.num_programs(0)

    def row_copy(slot, k, n, src_row):
        src = ys_ref.at[pl.ds(pl.multiple_of(src_row, s_rows), s_rows)]
        dst = buf.at[slot, k, pl.ds(pl.multiple_of(n * s_rows, s_rows), s_rows)]
        return pltpu.make_async_copy(src, dst, sem.at[slot])

    def issue(d_ref, slot):
        def body(n, carry):
            for k in range(TOP_K):
                row_copy(slot, k, n, d_ref[k * tile + n]).start(priority=k % 2)
            return carry
        lax.fori_loop(0, tile, body, 0)

    @pl.when(s == 0)
    def _():
        issue(dcur_ref, 0)

    @pl.when(s + 1 < n_steps)
    def _():
        issue(dnext_ref, (s + 1) % 2)

    def consume(slot):
        def wait_body(n, carry):
            for k in range(TOP_K):
                row_copy(slot, k, 0, 0).wait()
            return carry
        lax.fori_loop(0, tile, wait_body, 0)
        gs = [gates_ref[:, k:k + 1] for k in range(TOP_K)]
        d = out_ref.shape[1]
        half = d // 2
        ssq = jnp.zeros((tile, 1), F32)
        for c in range(s_rows):
            lo_cols = slice(c * LANES, (c + 1) * LANES)
            hi_cols = slice(half + c * LANES, half + (c + 1) * LANES)
            acc_lo = h_ref[:, lo_cols]
            acc_hi = h_ref[:, hi_cols]
            for k in range(TOP_K):
                lo, hi = _load_packed_rows(buf.at[slot, k], c, tile, s_rows)
                acc_lo = acc_lo + gs[k] * lo
                acc_hi = acc_hi + gs[k] * hi
            out_ref[:, lo_cols] = acc_lo
            out_ref[:, hi_cols] = acc_hi
            ssq = ssq + jnp.sum(acc_lo * acc_lo + acc_hi * acc_hi, axis=-1, keepdims=True)
        out_ref[...] = out_ref[...] * lax.rsqrt(ssq / d + EPS) * fg_ref[...]

    for slot in range(2):
        @pl.when(s % 2 == slot)
        def _(slot=slot):
            consume(slot)


def _combine(src_rows_tiles, ys, h2, gates_t, fg, *, tile):
    n, d = h2.shape
    s_rows = d // (2 * LANES)
    n_steps = n // tile
    return pl.pallas_call(
        functools.partial(_combine_kernel, tile=tile, s_rows=s_rows),
        grid=(n_steps,),
        in_specs=[
            pl.BlockSpec((TOP_K * tile,), lambda i: (i,), memory_space=pltpu.SMEM),
            pl.BlockSpec((TOP_K * tile,), lambda i: (jnp.minimum(i + 1, n_steps - 1),), memory_space=pltpu.SMEM),
            pl.BlockSpec(memory_space=pl.ANY),
            pl.BlockSpec((tile, d), lambda i: (i, 0)),
            pl.BlockSpec((tile, TOP_K), lambda i: (i, 0)),
            _const_spec((1, d)),
        ],
        out_specs=pl.BlockSpec((tile, d), lambda i: (i, 0)),
        out_shape=jax.ShapeDtypeStruct((n, d), F32),
        scratch_shapes=[pltpu.VMEM((2, TOP_K, tile * s_rows, LANES), U32), pltpu.SemaphoreType.DMA((2,))],
        compiler_params=pltpu.CompilerParams(
            dimension_semantics=("arbitrary",), vmem_limit_bytes=VMEM_LIMIT_BYTES,
            disable_bounds_checks=True),
        name="combine",
    )(src_rows_tiles, src_rows_tiles, ys, h2, gates_t, fg)


def _tile_major(a, tile):
    k, n = a.shape
    return a.reshape(k, n // tile, tile).transpose(1, 0, 2).reshape(-1)


def _layer(h, mem, ln_mix_g, w_in, conv_w, conv_b, w_rg_a, b_rg_a, w_rg_i, b_rg_i, lru_lambda, gm_v_norm_g,
           w_spatial, b_spatial, mem_norm_g, w_mem_k, w_mem_v, out_norm_g, w_out, ln_ffn_g, w_router, b_router,
           w_gate_up, b_gate_up, w_down, b_down, final_g, *, mix_tile, moe_tile, up_tn, down_rows,
           dispatch_tile, combine_tile):
    b, t, d = h.shape
    n = b * t
    n_exp = w_router.shape[1]
    row = lambda a: a.reshape(1, -1)
    km, v = _mem_kv(mem, row(mem_norm_g), w_mem_k.astype(BF16), w_mem_v.astype(BF16))
    prm = dict(
        lnmix=row(ln_mix_g), w_in=w_in.astype(BF16), conv_w=conv_w, conv_b=row(conv_b),
        wa=w_rg_a.astype(BF16), ba=row(b_rg_a), wi=w_rg_i.astype(BF16), bi=row(b_rg_i), lam=row(lru_lambda),
        gvn=row(gm_v_norm_g), wsp=w_spatial, bst=b_spatial.T, outg=row(out_norm_g), w_out=w_out.astype(BF16),
        lnffn=row(ln_ffn_g), wrt=w_router.T, br=b_router.reshape(-1, 1))
    h1, hnp, idx, gates, rank, cnt = _mixer(h, km, v, prm, tt=mix_tile)

    counts = cnt[:, 0].astype(I32)
    padded = ((counts + moe_tile - 1) // moe_tile) * moe_tile
    pad_end = jnp.cumsum(padded)
    pad_start = pad_end - padded
    s_rows = d // (2 * LANES)
    experts = jnp.arange(n_exp, dtype=I32)
    start_of = jnp.sum(jnp.where(idx[..., None] == experts, pad_start, 0), axis=-1)
    dest_rows = (start_of + rank) * s_rows
    nb = (n * TOP_K) // moe_tile + n_exp
    n_active = (pad_end[-1] // moe_tile).astype(I32).reshape(1)
    block_first = jnp.arange(nb, dtype=I32) * moe_tile
    block_expert = jnp.minimum(
        jnp.sum((pad_end[None, :] <= block_first[:, None]).astype(I32), axis=-1), n_exp - 1)

    xs = _dispatch(pad_start + counts, padded - counts, n_active, _tile_major(dest_rows, dispatch_tile), hnp,
                   nb * moe_tile, tile=dispatch_tile, tm=moe_tile, s_rows=s_rows)
    f = w_down.shape[1]
    r_ = jnp.arange(MXU_DIM)
    src = jnp.where(r_ < MXU_DIM // 2, 2 * r_, 2 * (r_ - MXU_DIM // 2) + 1)
    perm = (jnp.arange(MXU_DIM)[:, None] == src[None, :]).astype(BF16)
    bg = b_gate_up[:, 0::2].reshape(n_exp, 1, f)
    bu = b_gate_up[:, 1::2].reshape(n_exp, 1, f)
    act = _moe_up(block_expert, n_active, xs, w_gate_up, bg, bu, perm, tm=moe_tile, tn=up_tn)
    ys = _moe_down(block_expert, n_active, act, w_down, b_down.reshape(n_exp, 1, d), tm=moe_tile,
                   sub_rows=down_rows)
    out = _combine(_tile_major(dest_rows, combine_tile), ys, h1.reshape(n, d), gates.T, row(final_g),
                   tile=combine_tile)
    return out.reshape(b, t, d)


def kernel(x, mem, ln_mix_g, w_in, conv_w, conv_b, w_rg_a, b_rg_a, w_rg_i, b_rg_i, lru_lambda, gm_v_norm_g, w_spatial, b_spatial, mem_norm_g, w_mem_k, w_mem_v, out_norm_g, w_out, ln_ffn_g, w_router, b_router, w_gate_up, b_gate_up, w_down, b_down, final_norm_g):
    depth = w_in.shape[0]
    assert depth == 1, "the final RMSNorm is fused into the single layer's combine stage"
    return _layer(
        x, mem, ln_mix_g[0], w_in[0], conv_w[0], conv_b[0], w_rg_a[0], b_rg_a[0], w_rg_i[0], b_rg_i[0],
        lru_lambda[0], gm_v_norm_g[0], w_spatial[0], b_spatial[0], mem_norm_g[0], w_mem_k[0], w_mem_v[0],
        out_norm_g[0], w_out[0], ln_ffn_g[0], w_router[0], b_router[0], w_gate_up[0], b_gate_up[0], w_down[0],
        b_down[0], final_norm_g,
        mix_tile=MIX_TILE, moe_tile=MOE_TILE, up_tn=UP_TN, down_rows=DOWN_ROWS,
        dispatch_tile=DISPATCH_TILE, combine_tile=COMBINE_TILE)
```

```python
import functools

import jax
import jax.numpy as jnp
from jax import lax
from jax.experimental import pallas as pl
from jax.experimental.pallas import tpu as pltpu

F32 = jnp.float32
BF16 = jnp.bfloat16
I32 = jnp.int32
U32 = jnp.uint32

HEAD_DIM = 128
RG_C = 8.0
TOP_K = 4
SWIGLU_LIMIT = 7.0
SWIGLU_ALPHA = 1.702
EPS = 1e-6

LANES = 128
SUBLANES = 8
MXU_DIM = 256
VMEM_LIMIT_BYTES = 56 * 1024 * 1024

MIX_TILE = 256
MOE_TILE = 512
UP_TN = 2048
DOWN_ROWS = 512
ROW_CHUNK = 128
DISPATCH_TILE = 512
DISPATCH_LAG = 64
COMBINE_TILE = 256
ISSUE_UNROLL = 4


def _rms(x, g):
    return x * lax.rsqrt(jnp.mean(x * x, axis=-1, keepdims=True) + EPS) * g


def _gelu(x):
    return x * (0.5 * (1.0 + jnp.tanh(0.7978845608028654 * (x + 0.044715 * (x * x * x)))))


def _sigmoid(x):
    return 1.0 / (1.0 + jnp.exp(-x))


def _const_spec(shape):
    nd = len(shape)
    return pl.BlockSpec(shape, lambda *_: (0,) * nd, pipeline_mode=pl.Buffered(1))


def _store_packed_rows(ref, y, s_rows, row0=0):
    rows, d = y.shape
    half = d // 2
    for c in range(s_rows):
        lo = y[:, c * LANES:(c + 1) * LANES].astype(BF16).astype(F32)
        hi = y[:, half + c * LANES:half + (c + 1) * LANES].astype(BF16).astype(F32)
        word = (pltpu.bitcast(lo, U32) >> 16) | (pltpu.bitcast(hi, U32) & jnp.uint32(0xFFFF0000))
        ref[pl.ds(row0 * s_rows + c, rows, stride=s_rows), :] = word


def _load_packed_rows(ref, c, rows, s_rows, row0=0):
    word = ref[pl.ds(row0 * s_rows + c, rows, stride=s_rows), :]
    return pltpu.bitcast(word << 16, F32), pltpu.bitcast(word & jnp.uint32(0xFFFF0000), F32)


def _mem_kv_kernel(mem_ref, g_ref, wk_ref, wv_ref, k_ref, v_ref):
    mn = _rms(mem_ref[0], g_ref[...]).astype(BF16)
    k_ref[0] = jnp.dot(mn, wk_ref[...], preferred_element_type=F32).astype(BF16)
    v_ref[0] = jnp.dot(mn, wv_ref[...], preferred_element_type=F32).astype(BF16)


def _mem_kv(mem, g, wk, wv):
    b, m, d = mem.shape
    xa = wk.shape[1]
    return pl.pallas_call(
        _mem_kv_kernel,
        grid=(b,),
        in_specs=[
            pl.BlockSpec((1, m, d), lambda i: (i, 0, 0)),
            _const_spec((1, d)),
            _const_spec((d, xa)),
            _const_spec((d, xa)),
        ],
        out_specs=[
            pl.BlockSpec((1, m, xa), lambda i: (i, 0, 0)),
            pl.BlockSpec((1, m, xa), lambda i: (i, 0, 0)),
        ],
        out_shape=[jax.ShapeDtypeStruct((b, m, xa), BF16), jax.ShapeDtypeStruct((b, m, xa), BF16)],
        compiler_params=pltpu.CompilerParams(
            dimension_semantics=("arbitrary",), vmem_limit_bytes=VMEM_LIMIT_BYTES),
        name="mem_kv",
    )(mem, g, wk, wv)


def _route_previous_tile(step, hnb_s, wrt_ref, br_ref, run_s, idx_ref, gate_ref, rank_ref, cnt_ref):
    n_exp = wrt_ref.shape[0]
    tt = hnb_s.shape[0]
    lg = lax.dot_general(wrt_ref[...], hnb_s[...], (((1,), (1,)), ((), ())),
                         preferred_element_type=F32) + br_ref[...]
    eid = lax.broadcasted_iota(I32, (n_exp, tt), 0)
    vals, sels = [], []
    for k in range(TOP_K):
        m = jnp.max(lg, axis=0, keepdims=True)
        ik = jnp.min(jnp.where(lg == m, eid, n_exp), axis=0, keepdims=True)
        sel = eid == ik
        vals.append(m)
        sels.append(sel)
        idx_ref[k:k + 1, :] = ik
        lg = jnp.where(sel, -jnp.inf, lg)
    es = [jnp.exp(v - vals[0]) for v in vals]
    den = es[0] + es[1] + es[2] + es[3]
    for k in range(TOP_K):
        gate_ref[k:k + 1, :] = es[k] / den
    oh = jnp.where(sels[0] | sels[1] | sels[2] | sels[3], 1.0, 0.0)
    upper = jnp.where(lax.broadcasted_iota(I32, (tt, tt), 0) < lax.broadcasted_iota(I32, (tt, tt), 1),
                      1.0, 0.0).astype(BF16)
    before = jnp.dot(oh.astype(BF16), upper, preferred_element_type=F32) + run_s[:, 0:1]
    for k in range(TOP_K):
        rank_ref[k:k + 1, :] = jnp.sum(jnp.where(sels[k], before, 0.0), axis=0, keepdims=True).astype(I32)
    counted = jnp.where(step >= 1, 1.0, 0.0)
    run_s[...] = run_s[...] + counted * jnp.sum(oh, axis=1, keepdims=True)
    cnt_ref[...] = run_s[...]


def _mixer_kernel(x_ref, lnmix_ref, win_ref, cw_ref, cb_ref, wai_ref, ba_ref, bi_ref, lam_ref,
                  gvn_ref, wsp_ref, bst_ref, k_ref, v_ref, outg_ref, wout_ref, lnffn_ref, wrt_ref, br_ref,
                  h_ref, hnp_ref, idx_ref, gate_ref, rank_ref, cnt_ref,
                  p_s, xr_s, a_s, u_s, hs_s, hc_s, y_s, yb_s, run_s, hnb_s,
                  *, rg, gm, xa, chunk, n_t):
    tt, d = x_ref.shape
    c1, c2, c3, c4 = rg, 2 * rg, 2 * rg + gm, 2 * rg + 2 * gm
    step = pl.program_id(0)

    @pl.when(step == 0)
    def _():
        run_s[...] = jnp.zeros(run_s.shape, F32)
        hnb_s[...] = jnp.zeros(hnb_s.shape, BF16)

    @pl.when(lax.rem(step, n_t) == 0)
    def _():
        xr_s[0:SUBLANES, :] = jnp.zeros((SUBLANES, rg), F32)
        hc_s[...] = jnp.zeros((SUBLANES, rg), F32)

    _route_previous_tile(step, hnb_s, wrt_ref, br_ref, run_s, idx_ref, gate_ref, rank_ref, cnt_ref)

    x = x_ref[...]
    xn = _rms(x, lnmix_ref[...]).astype(BF16)
    p_s[...] = jnp.dot(xn, win_ref[...], preferred_element_type=F32)

    xr_s[SUBLANES:SUBLANES + tt, :] = p_s[:, 0:c1]
    cw = cw_ref[...]
    kw = cw.shape[0]
    xc = cb_ref[...] + cw[0:1, :] * xr_s[pl.ds(SUBLANES - kw + 1, tt), :]
    for w in range(1, kw):
        xc = xc + cw[w:w + 1, :] * xr_s[pl.ds(SUBLANES - kw + 1 + w, tt), :]
    xr_s[0:SUBLANES, :] = xr_s[tt:tt + SUBLANES, :]

    z = -lam_ref[...]
    sp = jnp.maximum(z, 0.0) + jnp.log1p(jnp.exp(-jnp.abs(z)))
    gates = jnp.dot(xc.astype(BF16), wai_ref[...], preferred_element_type=F32)
    for hd in range(rg // HEAD_DIM):
        sl = slice(hd * HEAD_DIM, (hd + 1) * HEAD_DIM)
        r = _sigmoid(gates[:, hd * HEAD_DIM:(hd + 1) * HEAD_DIM] + ba_ref[:, sl])
        ig = _sigmoid(gates[:, rg + hd * HEAD_DIM:rg + (hd + 1) * HEAD_DIM] + bi_ref[:, sl])
        log_a = (-RG_C) * r * sp[:, sl]
        a = jnp.exp(log_a)
        a_s[:, sl] = a
        u_s[:, sl] = jnp.sqrt(-jnp.tanh(log_a) * (a * a + 1.0)) * (ig * xc[:, sl])

    row = lax.broadcasted_iota(I32, (SUBLANES, rg), 0)
    hprev = hc_s[...]
    for g in range(tt // SUBLANES):
        rows = slice(g * SUBLANES, (g + 1) * SUBLANES)
        aa = a_s[rows, :]
        bb = u_s[rows, :]
        for s in (1, 2, 4):
            keep = row >= s
            bb = jnp.where(keep, aa * pltpu.roll(bb, s, 0) + bb, bb)
            aa = jnp.where(keep, aa * pltpu.roll(aa, s, 0), aa)
        h8 = aa * hprev + bb
        hs_s[rows, :] = h8
        hprev = jnp.broadcast_to(h8[SUBLANES - 1:SUBLANES, :], (SUBLANES, rg))
    hc_s[...] = hprev
    y_s[:, 0:rg] = _gelu(p_s[:, c1:c2]) * hs_s[...]

    vn = _rms(_gelu(p_s[:, c3:c4]), gvn_ref[...]).astype(BF16)
    tri = (lax.broadcasted_iota(I32, (chunk, chunk), 0) >= lax.broadcasted_iota(I32, (chunk, chunk), 1))
    for g in range(gm // HEAD_DIM):
        wg = jnp.where(tri, wsp_ref[g], 0.0).astype(BF16)
        bcol = bst_ref[:, g:g + 1]
        for c in range(tt // chunk):
            rs = slice(c * chunk, (c + 1) * chunk)
            cs = slice(g * HEAD_DIM, (g + 1) * HEAD_DIM)
            sv = jnp.dot(wg, vn[rs, cs], preferred_element_type=F32) + bcol
            y_s[rs, rg + g * HEAD_DIM:rg + (g + 1) * HEAD_DIM] = (
                _gelu(p_s[rs, c2 + g * HEAD_DIM:c2 + (g + 1) * HEAD_DIM]) * sv)

    scale = HEAD_DIM ** -0.5
    for hd in range(xa // HEAD_DIM):
        sl = slice(hd * HEAD_DIM, (hd + 1) * HEAD_DIM)
        q = p_s[:, c4 + hd * HEAD_DIM:c4 + (hd + 1) * HEAD_DIM].astype(BF16)
        s = lax.dot_general(q, k_ref[0, :, sl], (((1,), (1,)), ((), ())),
                            preferred_element_type=F32) * scale
        e = jnp.exp(s - jnp.max(s, axis=-1, keepdims=True))
        o = jnp.dot(e.astype(BF16), v_ref[0, :, sl], preferred_element_type=F32)
        y_s[:, rg + gm + hd * HEAD_DIM:rg + gm + (hd + 1) * HEAD_DIM] = o / jnp.sum(e, axis=-1, keepdims=True)

    for lo, hi in ((0, rg), (rg, rg + gm), (rg + gm, rg + gm + xa)):
        yb_s[:, lo:hi] = _rms(y_s[:, lo:hi], outg_ref[:, lo:hi]).astype(BF16)
    h = x + jnp.dot(yb_s[...], wout_ref[...], preferred_element_type=F32)
    h_ref[...] = h

    hn = _rms(h, lnffn_ref[...])
    _store_packed_rows(hnp_ref, hn, d // (2 * LANES))
    hnb_s[...] = hn.astype(BF16)


def _mixer(x, k, v, prm, *, tt):
    b, t, d = x.shape
    n_t = t // tt
    n = b * t
    rg = prm["conv_w"].shape[1]
    gm = prm["gvn"].shape[1]
    m = k.shape[1]
    xa = k.shape[2]
    chunk = prm["wsp"].shape[1]
    in_cols = prm["w_in"].shape[1]
    n_exp = prm["wrt"].shape[0]
    s_rows = d // (2 * LANES)
    const_names = ("lnmix", "w_in", "conv_w", "conv_b", "wai", "ba", "bi", "lam", "gvn", "wsp", "bst")
    const_names2 = ("outg", "w_out", "lnffn", "wrt", "br")
    total = b * n_t
    tile_of = lambda s: jnp.minimum(s, total - 1)
    in_specs = ([pl.BlockSpec((tt, d), lambda s: (tile_of(s), 0))]
                + [_const_spec(prm[c].shape) for c in const_names]
                + [pl.BlockSpec((1, m, xa), lambda s: (tile_of(s) // n_t, 0, 0)),
                   pl.BlockSpec((1, m, xa), lambda s: (tile_of(s) // n_t, 0, 0))]
                + [_const_spec(prm[c].shape) for c in const_names2])
    routed = lambda s: (0, jnp.maximum(s - 1, 0))
    out_specs = [
        pl.BlockSpec((tt, d), lambda s: (s, 0)),
        pl.BlockSpec((tt * s_rows, LANES), lambda s: (s, 0)),
        pl.BlockSpec((TOP_K, tt), routed),
        pl.BlockSpec((TOP_K, tt), routed),
        pl.BlockSpec((TOP_K, tt), routed),
        pl.BlockSpec((n_exp, LANES), lambda s: (0, 0)),
    ]
    out_shape = [
        jax.ShapeDtypeStruct((n + tt, d), F32),
        jax.ShapeDtypeStruct(((n + tt) * s_rows, LANES), U32),
        jax.ShapeDtypeStruct((TOP_K, n), I32),
        jax.ShapeDtypeStruct((TOP_K, n), F32),
        jax.ShapeDtypeStruct((TOP_K, n), I32),
        jax.ShapeDtypeStruct((n_exp, LANES), F32),
    ]
    scratch = [
        pltpu.VMEM((tt, in_cols), F32),
        pltpu.VMEM((tt + 2 * SUBLANES, rg), F32),
        pltpu.VMEM((tt, rg), F32),
        pltpu.VMEM((tt, rg), F32),
        pltpu.VMEM((tt, rg), F32),
        pltpu.VMEM((SUBLANES, rg), F32),
        pltpu.VMEM((tt, rg + gm + xa), F32),
        pltpu.VMEM((tt, rg + gm + xa), BF16),
        pltpu.VMEM((n_exp, LANES), F32),
        pltpu.VMEM((tt, d), BF16),
    ]
    return pl.pallas_call(
        functools.partial(_mixer_kernel, rg=rg, gm=gm, xa=xa, chunk=chunk, n_t=n_t),
        grid=(total + 1,),
        in_specs=in_specs,
        out_specs=out_specs,
        out_shape=out_shape,
        scratch_shapes=scratch,
        compiler_params=pltpu.CompilerParams(
            dimension_semantics=("arbitrary",), vmem_limit_bytes=VMEM_LIMIT_BYTES),
        name="mixer",
    )(x.reshape(n, d), *[prm[c] for c in const_names], k, v, *[prm[c] for c in const_names2])


def _dispatch_kernel(zs_ref, zl_ref, na_ref, dest_ref, hnp_ref, xs_ref, zb, sem, zsem, *, tile, lag, tm, s_rows):
    step = pl.program_id(0)
    n_exp = zs_ref.shape[0]
    nb = xs_ref.shape[0] // (tm * s_rows)

    def for_each_zero_copy(act):
        def per_expert(e, carry):
            off = zs_ref[e]
            length = zl_ref[e]
            p = tm // 2
            while p >= 1:
                bit = length & p

                @pl.when(bit != 0)
                def _(off=off, p=p):
                    dst = xs_ref.at[pl.ds(pl.multiple_of(off * s_rows, s_rows), p * s_rows)]
                    act(pltpu.make_async_copy(zb.at[pl.ds(0, p * s_rows)], dst, zsem))
                off = off + bit
                p //= 2
            return carry
        lax.fori_loop(0, n_exp, per_expert, 0)

        def per_block(b, carry):
            dst = xs_ref.at[pl.ds(pl.multiple_of(b * (tm * s_rows), tm * s_rows), tm * s_rows)]
            act(pltpu.make_async_copy(zb, dst, zsem))
            return carry
        lax.fori_loop(na_ref[0], nb, per_block, 0)

    @pl.when(step == 0)
    def _():
        zb[...] = jnp.zeros(zb.shape, U32)
        for_each_zero_copy(lambda c: c.start())

    def row_copy(n, dst_row):
        src = hnp_ref.at[pl.ds(pl.multiple_of(n * s_rows, s_rows), s_rows)]
        return pltpu.make_async_copy(src, xs_ref.at[pl.ds(pl.multiple_of(dst_row, s_rows), s_rows)], sem)

    def wait_token():
        for _ in range(TOP_K):
            row_copy(0, 0).wait()

    def body(n, carry):
        for k in range(TOP_K):
            row_copy(n, dest_ref[k * tile + n]).start(priority=k % 2)

        @pl.when(n >= lag)
        def _():
            wait_token()
        return carry

    lax.fori_loop(0, tile, body, 0)

    def drain(_, carry):
        wait_token()
        return carry

    lax.fori_loop(0, lag, drain, 0)

    @pl.when(step == 0)
    def _():
        for_each_zero_copy(lambda c: c.wait())


def _dispatch(zero_start, zero_len, n_active, dest_rows_tiles, hnp, rows, *, tile, tm, s_rows):
    n = dest_rows_tiles.shape[0] // TOP_K
    lag = min(DISPATCH_LAG, tile)
    grid_spec = pltpu.PrefetchScalarGridSpec(
        num_scalar_prefetch=3,
        grid=(n // tile,),
        in_specs=[
            pl.BlockSpec((TOP_K * tile,), lambda i, *_: (i,), memory_space=pltpu.SMEM),
            pl.BlockSpec((tile * s_rows, LANES), lambda i, *_: (i, 0)),
        ],
        out_specs=pl.BlockSpec(memory_space=pl.ANY),
        scratch_shapes=[pltpu.VMEM((tm * s_rows, LANES), U32), pltpu.SemaphoreType.DMA(()),
                        pltpu.SemaphoreType.DMA(())],
    )
    return pl.pallas_call(
        functools.partial(_dispatch_kernel, tile=tile, lag=lag, tm=tm, s_rows=s_rows),
        grid_spec=grid_spec,
        out_shape=jax.ShapeDtypeStruct((rows * s_rows, LANES), U32),
        compiler_params=pltpu.CompilerParams(
            dimension_semantics=("arbitrary",), vmem_limit_bytes=VMEM_LIMIT_BYTES,
            disable_bounds_checks=True, has_side_effects=True),
        name="dispatch",
    )(zero_start, zero_len, n_active, dest_rows_tiles, hnp)


def _expert_changed(be_ref, i):
    prev = be_ref[jnp.maximum(i - 1, 0)]
    return (i == 0) | (be_ref[i] != prev)


def _moe_up_kernel(be_ref, na_ref, xs_ref, w_ref, bg_ref, bu_ref, perm_ref, act_ref, wp_s, xb_s):
    i = pl.program_id(1)
    tn = w_ref.shape[2]
    hn_ = tn // 2
    d = xb_s.shape[1]
    active = i < na_ref[0]

    @pl.when(active & _expert_changed(be_ref, i))
    def _():
        for g in range(tn // MXU_DIM):
            wg = w_ref[0, :, g * MXU_DIM:(g + 1) * MXU_DIM].astype(BF16)
            wq = jnp.dot(wg, perm_ref[...], preferred_element_type=F32).astype(BF16)
            hw = MXU_DIM // 2
            wp_s[:, g * hw:(g + 1) * hw] = wq[:, 0:hw]
            wp_s[:, hn_ + g * hw:hn_ + (g + 1) * hw] = wq[:, hw:MXU_DIM]

    @pl.when(active)
    def _():
        half = d // 2
        s_rows = half // LANES
        tm = xb_s.shape[0]
        rc = min(ROW_CHUNK, tm)
        for r0 in range(0, tm, rc):
            rows = slice(r0, r0 + rc)
            for c in range(s_rows):
                lo, hi = _load_packed_rows(xs_ref, c, rc, s_rows, row0=r0)
                xb_s[rows, c * LANES:(c + 1) * LANES] = lo.astype(BF16)
                xb_s[rows, half + c * LANES:half + (c + 1) * LANES] = hi.astype(BF16)
            gu = jnp.dot(xb_s[rows, :], wp_s[...], preferred_element_type=F32)
            gate = jnp.minimum(gu[:, 0:hn_] + bg_ref[0], SWIGLU_LIMIT)
            up = jnp.clip(gu[:, hn_:tn] + bu_ref[0], -SWIGLU_LIMIT, SWIGLU_LIMIT)
            act_ref[rows, :] = ((up + 1.0) * (gate * _sigmoid(SWIGLU_ALPHA * gate))).astype(BF16)

    @pl.when(jnp.logical_not(active))
    def _():
        act_ref[...] = jnp.zeros(act_ref.shape, BF16)


def _moe_up(block_expert, n_active, xs, w_gate_up, bg, bu, perm, *, tm, tn):
    n_exp, d, f2 = w_gate_up.shape
    s_rows = d // (2 * LANES)
    rows = xs.shape[0] // s_rows
    nb = rows // tm

    def blk(i, na):
        return jnp.minimum(i, na[0] - 1)

    grid_spec = pltpu.PrefetchScalarGridSpec(
        num_scalar_prefetch=2,
        grid=(f2 // tn, nb),
        in_specs=[
            pl.BlockSpec((tm * s_rows, LANES), lambda j, i, be, na: (blk(i, na), 0)),
            pl.BlockSpec((1, d, tn), lambda j, i, be, na: (be[blk(i, na)], 0, j)),
            pl.BlockSpec((1, 1, tn // 2), lambda j, i, be, na: (be[blk(i, na)], 0, j)),
            pl.BlockSpec((1, 1, tn // 2), lambda j, i, be, na: (be[blk(i, na)], 0, j)),
            pl.BlockSpec((MXU_DIM, MXU_DIM), lambda j, i, be, na: (0, 0)),
        ],
        out_specs=pl.BlockSpec((tm, tn // 2), lambda j, i, be, na: (i, j)),
        scratch_shapes=[pltpu.VMEM((d, tn), BF16), pltpu.VMEM((tm, d), BF16)],
    )
    return pl.pallas_call(
        _moe_up_kernel,
        grid_spec=grid_spec,
        out_shape=jax.ShapeDtypeStruct((rows, f2 // 2), BF16),
        compiler_params=pltpu.CompilerParams(
            dimension_semantics=("arbitrary", "arbitrary"), vmem_limit_bytes=VMEM_LIMIT_BYTES),
        name="moe_up",
    )(block_expert, n_active, xs, w_gate_up, bg, bu, perm)


def _moe_down_kernel(be_ref, na_ref, act_ref, w_ref, b_ref, ys_ref, wb_s, *, sub_shift):
    i = pl.program_id(0)
    blk = lax.shift_right_logical(i, sub_shift)
    active = blk < na_ref[0]
    first_sub = (i & ((1 << sub_shift) - 1)) == 0

    @pl.when(active & first_sub & _expert_changed(be_ref, blk))
    def _():
        wb_s[...] = w_ref[0].astype(BF16)

    @pl.when(active)
    def _():
        n_rows = act_ref.shape[0]
        rc = min(ROW_CHUNK, n_rows)
        for r0 in range(0, n_rows, rc):
            y = jnp.dot(act_ref[r0:r0 + rc, :], wb_s[...], preferred_element_type=F32) + b_ref[0]
            _store_packed_rows(ys_ref, y, y.shape[1] // (2 * LANES), row0=r0)

    @pl.when(jnp.logical_not(active))
    def _():
        ys_ref[...] = jnp.zeros(ys_ref.shape, U32)


def _moe_down(block_expert, n_active, act, w_down, b_down, *, tm, sub_rows):
    rows, f = act.shape
    n_exp, _, d = w_down.shape
    s_rows = d // (2 * LANES)
    sub = tm // sub_rows
    sub_shift = sub.bit_length() - 1
    assert (1 << sub_shift) == sub and sub * sub_rows == tm
    n_steps = rows // sub_rows

    def blk(i, na):
        return jnp.minimum(lax.shift_right_logical(i, sub_shift), na[0] - 1)

    grid_spec = pltpu.PrefetchScalarGridSpec(
        num_scalar_prefetch=2,
        grid=(n_steps,),
        in_specs=[
            pl.BlockSpec((sub_rows, f), lambda i, be, na: (jnp.minimum(i, na[0] * sub - 1), 0)),
            pl.BlockSpec((1, f, d), lambda i, be, na: (be[blk(i, na)], 0, 0)),
            pl.BlockSpec((1, 1, d), lambda i, be, na: (be[blk(i, na)], 0, 0)),
        ],
        out_specs=pl.BlockSpec((sub_rows * s_rows, LANES), lambda i, be, na: (i, 0)),
        scratch_shapes=[pltpu.VMEM((f, d), BF16)],
    )
    return pl.pallas_call(
        functools.partial(_moe_down_kernel, sub_shift=sub_shift),
        grid_spec=grid_spec,
        out_shape=jax.ShapeDtypeStruct((rows * s_rows, LANES), U32),
        compiler_params=pltpu.CompilerParams(
            dimension_semantics=("arbitrary",), vmem_limit_bytes=VMEM_LIMIT_BYTES),
        name="moe_down",
    )(block_expert, n_active, act, w_down, b_down)


def _combine_kernel(dcur_ref, dnext_ref, ys_ref, h_ref, gates_ref, fg_ref, out_ref, buf, sem, *, tile, s_rows):
    s = pl.program_id(0)
    n_steps = pl.num_programs(0)

    slot_rows = TOP_K * tile * s_rows

    def row_copy(slot, k, n, src_row):
        src = ys_ref.at[pl.ds(pl.multiple_of(src_row, s_rows), s_rows)]
        dst = buf.at[slot, pl.ds(pl.multiple_of((k * tile + n) * s_rows, s_rows), s_rows)]
        return pltpu.make_async_copy(src, dst, sem.at[slot])

    def issue(d_ref, slot):
        def body(n, carry):
            for k in range(TOP_K):
                row_copy(slot, k, n, d_ref[k * tile + n]).start(priority=k % 2)
            return carry
        lax.fori_loop(0, tile, body, 0, unroll=ISSUE_UNROLL)

    @pl.when(s == 0)
    def _():
        issue(dcur_ref, 0)

    @pl.when(s + 1 < n_steps)
    def _():
        issue(dnext_ref, (s + 1) % 2)

    def consume(slot):
        pltpu.make_async_copy(ys_ref.at[pl.ds(0, slot_rows)], buf.at[slot], sem.at[slot]).wait()
        gs = [gates_ref[:, k:k + 1] for k in range(TOP_K)]
        d = out_ref.shape[1]
        half = d // 2
        ssq = jnp.zeros((tile, 1), F32)
        for c in range(s_rows):
            lo_cols = slice(c * LANES, (c + 1) * LANES)
            hi_cols = slice(half + c * LANES, half + (c + 1) * LANES)
            acc_lo = h_ref[:, lo_cols]
            acc_hi = h_ref[:, hi_cols]
            for k in range(TOP_K):
                lo, hi = _load_packed_rows(buf.at[slot], c, tile, s_rows, row0=k * tile)
                acc_lo = acc_lo + gs[k] * lo
                acc_hi = acc_hi + gs[k] * hi
            out_ref[:, lo_cols] = acc_lo
            out_ref[:, hi_cols] = acc_hi
            ssq = ssq + jnp.sum(acc_lo * acc_lo + acc_hi * acc_hi, axis=-1, keepdims=True)
        out_ref[...] = out_ref[...] * lax.rsqrt(ssq / d + EPS) * fg_ref[...]

    for slot in range(2):
        @pl.when(s % 2 == slot)
        def _(slot=slot):
            consume(slot)


def _combine(src_rows_tiles, ys, h2, gates_t, fg, *, n, tile):
    d = h2.shape[1]
    s_rows = d // (2 * LANES)
    n_steps = n // tile
    return pl.pallas_call(
        functools.partial(_combine_kernel, tile=tile, s_rows=s_rows),
        grid=(n_steps,),
        in_specs=[
            pl.BlockSpec((TOP_K * tile,), lambda i: (i,), memory_space=pltpu.SMEM),
            pl.BlockSpec((TOP_K * tile,), lambda i: (jnp.minimum(i + 1, n_steps - 1),), memory_space=pltpu.SMEM),
            pl.BlockSpec(memory_space=pl.ANY),
            pl.BlockSpec((tile, d), lambda i: (i, 0)),
            pl.BlockSpec((tile, TOP_K), lambda i: (i, 0)),
            _const_spec((1, d)),
        ],
        out_specs=pl.BlockSpec((tile, d), lambda i: (i, 0)),
        out_shape=jax.ShapeDtypeStruct((n, d), F32),
        scratch_shapes=[pltpu.VMEM((2, TOP_K * tile * s_rows, LANES), U32), pltpu.SemaphoreType.DMA((2,))],
        compiler_params=pltpu.CompilerParams(
            dimension_semantics=("arbitrary",), vmem_limit_bytes=VMEM_LIMIT_BYTES,
            disable_bounds_checks=True),
        name="combine",
    )(src_rows_tiles, src_rows_tiles, ys, h2, gates_t, fg)


def _tile_major(a, tile):
    k, n = a.shape
    return a.reshape(k, n // tile, tile).transpose(1, 0, 2).reshape(-1)


def _layer(h, mem, ln_mix_g, w_in, conv_w, conv_b, w_rg_a, b_rg_a, w_rg_i, b_rg_i, lru_lambda, gm_v_norm_g,
           w_spatial, b_spatial, mem_norm_g, w_mem_k, w_mem_v, out_norm_g, w_out, ln_ffn_g, w_router, b_router,
           w_gate_up, b_gate_up, w_down, b_down, final_g, *, mix_tile, moe_tile, up_tn, down_rows,
           dispatch_tile, combine_tile):
    b, t, d = h.shape
    n = b * t
    n_exp = w_router.shape[1]
    row = lambda a: a.reshape(1, -1)
    km, v = _mem_kv(mem, row(mem_norm_g), w_mem_k.astype(BF16), w_mem_v.astype(BF16))
    heads = w_rg_a.shape[0]
    eye = jnp.eye(heads, dtype=w_rg_a.dtype)

    def block_diag(w):
        return (eye[:, None, :, None] * w[:, :, None, :]).reshape(heads * w.shape[1], heads * w.shape[2])

    prm = dict(
        lnmix=row(ln_mix_g), w_in=w_in.astype(BF16), conv_w=conv_w, conv_b=row(conv_b),
        wai=jnp.concatenate([block_diag(w_rg_a), block_diag(w_rg_i)], axis=1).astype(BF16),
        ba=row(b_rg_a), bi=row(b_rg_i), lam=row(lru_lambda),
        gvn=row(gm_v_norm_g), wsp=w_spatial, bst=b_spatial.T, outg=row(out_norm_g), w_out=w_out.astype(BF16),
        lnffn=row(ln_ffn_g), wrt=w_router.T.astype(BF16), br=b_router.reshape(-1, 1))
    h1, hnp, idx, gates, rank, cnt = _mixer(h, km, v, prm, tt=mix_tile)

    counts = cnt[:, 0].astype(I32)
    padded = ((counts + moe_tile - 1) // moe_tile) * moe_tile
    pad_end = jnp.cumsum(padded)
    pad_start = pad_end - padded
    s_rows = d // (2 * LANES)
    experts = jnp.arange(n_exp, dtype=I32)
    start_of = jnp.sum(jnp.where(idx[..., None] == experts, pad_start, 0), axis=-1)
    dest_rows = (start_of + rank) * s_rows
    nb = (n * TOP_K) // moe_tile + n_exp
    n_active = (pad_end[-1] // moe_tile).astype(I32).reshape(1)
    block_first = jnp.arange(nb, dtype=I32) * moe_tile
    block_expert = jnp.minimum(
        jnp.sum((pad_end[None, :] <= block_first[:, None]).astype(I32), axis=-1), n_exp - 1)

    xs = _dispatch(pad_start + counts, padded - counts, n_active, _tile_major(dest_rows, dispatch_tile), hnp,
                   nb * moe_tile, tile=dispatch_tile, tm=moe_tile, s_rows=s_rows)
    f = w_down.shape[1]
    r_ = jnp.arange(MXU_DIM)
    src = jnp.where(r_ < MXU_DIM // 2, 2 * r_, 2 * (r_ - MXU_DIM // 2) + 1)
    perm = (jnp.arange(MXU_DIM)[:, None] == src[None, :]).astype(BF16)
    bg = b_gate_up[:, 0::2].reshape(n_exp, 1, f)
    bu = b_gate_up[:, 1::2].reshape(n_exp, 1, f)
    act = _moe_up(block_expert, n_active, xs, w_gate_up, bg, bu, perm, tm=moe_tile, tn=up_tn)
    ys = _moe_down(block_expert, n_active, act, w_down, b_down.reshape(n_exp, 1, d), tm=moe_tile,
                   sub_rows=down_rows)
    out = _combine(_tile_major(dest_rows, combine_tile), ys, h1, gates.T, row(final_g), n=n, tile=combine_tile)
    return out.reshape(b, t, d)


def kernel(x, mem, ln_mix_g, w_in, conv_w, conv_b, w_rg_a, b_rg_a, w_rg_i, b_rg_i, lru_lambda, gm_v_norm_g, w_spatial, b_spatial, mem_norm_g, w_mem_k, w_mem_v, out_norm_g, w_out, ln_ffn_g, w_router, b_router, w_gate_up, b_gate_up, w_down, b_down, final_norm_g):
    depth = w_in.shape[0]
    assert depth == 1, "the final RMSNorm is fused into the single layer's combine stage"
    return _layer(
        x, mem, ln_mix_g[0], w_in[0], conv_w[0], conv_b[0], w_rg_a[0], b_rg_a[0], w_rg_i[0], b_rg_i[0],
        lru_lambda[0], gm_v_norm_g[0], w_spatial[0], b_spatial[0], mem_norm_g[0], w_mem_k[0], w_mem_v[0],
        out_norm_g[0], w_out[0], ln_ffn_g[0], w_router[0], b_router[0], w_gate_up[0], b_gate_up[0], w_down[0],
        b_down[0], final_norm_g,
        mix_tile=MIX_TILE, moe_tile=MOE_TILE, up_tn=UP_TN, down_rows=DOWN_ROWS,
        dispatch_tile=DISPATCH_TILE, combine_tile=COMBINE_TILE)
```

```python
import functools

import jax
import jax.numpy as jnp
from jax import lax
from jax.experimental import pallas as pl
from jax.experimental.pallas import tpu as pltpu

F32 = jnp.float32
BF16 = jnp.bfloat16
I32 = jnp.int32
U32 = jnp.uint32

HEAD_DIM = 128
RG_C = 8.0
TOP_K = 4
SWIGLU_LIMIT = 7.0
SWIGLU_ALPHA = 1.702
EPS = 1e-6

LANES = 128
SUBLANES = 8
MXU_DIM = 256
VMEM_LIMIT_BYTES = 56 * 1024 * 1024

MIX_TILE = 256
MOE_TILE = 512
UP_TN = 2048
ROW_CHUNK = 128
DISPATCH_TILE = 512
DISPATCH_LAG = 64
COMBINE_TILE = 256
ISSUE_UNROLL = 4


def _rms(x, g):
    return x * lax.rsqrt(jnp.mean(x * x, axis=-1, keepdims=True) + EPS) * g


def _gelu(x):
    return x * (0.5 * (1.0 + jnp.tanh(0.7978845608028654 * (x + 0.044715 * (x * x * x)))))


def _sigmoid(x):
    return 1.0 / (1.0 + jnp.exp(-x))


def _const_spec(shape):
    nd = len(shape)
    return pl.BlockSpec(shape, lambda *_: (0,) * nd, pipeline_mode=pl.Buffered(1))


def _store_packed_rows(ref, y, s_rows, row0=0):
    rows, d = y.shape
    half = d // 2
    for c in range(s_rows):
        lo = y[:, c * LANES:(c + 1) * LANES].astype(BF16).astype(F32)
        hi = y[:, half + c * LANES:half + (c + 1) * LANES].astype(BF16).astype(F32)
        word = (pltpu.bitcast(lo, U32) >> 16) | (pltpu.bitcast(hi, U32) & jnp.uint32(0xFFFF0000))
        ref[pl.ds(row0 * s_rows + c, rows, stride=s_rows), :] = word


def _load_packed_rows(ref, c, rows, s_rows, row0=0):
    word = ref[pl.ds(row0 * s_rows + c, rows, stride=s_rows), :]
    return pltpu.bitcast(word << 16, F32), pltpu.bitcast(word & jnp.uint32(0xFFFF0000), F32)


def _mem_kv_kernel(mem_ref, g_ref, wk_ref, wv_ref, k_ref, v_ref):
    mn = _rms(mem_ref[0], g_ref[...]).astype(BF16)
    k_ref[0] = jnp.dot(mn, wk_ref[...], preferred_element_type=F32).astype(BF16)
    v_ref[0] = jnp.dot(mn, wv_ref[...], preferred_element_type=F32).astype(BF16)


def _mem_kv(mem, g, wk, wv):
    b, m, d = mem.shape
    xa = wk.shape[1]
    return pl.pallas_call(
        _mem_kv_kernel,
        grid=(b,),
        in_specs=[
            pl.BlockSpec((1, m, d), lambda i: (i, 0, 0)),
            _const_spec((1, d)),
            _const_spec((d, xa)),
            _const_spec((d, xa)),
        ],
        out_specs=[
            pl.BlockSpec((1, m, xa), lambda i: (i, 0, 0)),
            pl.BlockSpec((1, m, xa), lambda i: (i, 0, 0)),
        ],
        out_shape=[jax.ShapeDtypeStruct((b, m, xa), BF16), jax.ShapeDtypeStruct((b, m, xa), BF16)],
        compiler_params=pltpu.CompilerParams(
            dimension_semantics=("arbitrary",), vmem_limit_bytes=VMEM_LIMIT_BYTES),
        name="mem_kv",
    )(mem, g, wk, wv)


def _route_previous_tile(step, hnb_s, wrt_ref, br_ref, run_s, idx_ref, gate_ref, rank_ref, cnt_ref):
    n_exp = wrt_ref.shape[0]
    tt = hnb_s.shape[0]
    lg = lax.dot_general(wrt_ref[...], hnb_s[...], (((1,), (1,)), ((), ())),
                         preferred_element_type=F32) + br_ref[...]
    eid = lax.broadcasted_iota(I32, (n_exp, tt), 0)
    vals, sels = [], []
    for k in range(TOP_K):
        m = jnp.max(lg, axis=0, keepdims=True)
        ik = jnp.min(jnp.where(lg == m, eid, n_exp), axis=0, keepdims=True)
        sel = eid == ik
        vals.append(m)
        sels.append(sel)
        idx_ref[k:k + 1, :] = ik
        lg = jnp.where(sel, -jnp.inf, lg)
    es = [jnp.exp(v - vals[0]) for v in vals]
    den = es[0] + es[1] + es[2] + es[3]
    for k in range(TOP_K):
        gate_ref[k:k + 1, :] = es[k] / den
    oh = jnp.where(sels[0] | sels[1] | sels[2] | sels[3], 1.0, 0.0)
    upper = jnp.where(lax.broadcasted_iota(I32, (tt, tt), 0) < lax.broadcasted_iota(I32, (tt, tt), 1),
                      1.0, 0.0).astype(BF16)
    before = jnp.dot(oh.astype(BF16), upper, preferred_element_type=F32) + run_s[:, 0:1]
    for k in range(TOP_K):
        rank_ref[k:k + 1, :] = jnp.sum(jnp.where(sels[k], before, 0.0), axis=0, keepdims=True).astype(I32)
    counted = jnp.where(step >= 1, 1.0, 0.0)
    run_s[...] = run_s[...] + counted * jnp.sum(oh, axis=1, keepdims=True)
    cnt_ref[...] = run_s[...]


def _mixer_kernel(x_ref, lnmix_ref, win_ref, cw_ref, cb_ref, wai_ref, ba_ref, bi_ref, lam_ref,
                  gvn_ref, wsp_ref, bst_ref, k_ref, v_ref, outg_ref, wout_ref, lnffn_ref, wrt_ref, br_ref,
                  h_ref, hnp_ref, idx_ref, gate_ref, rank_ref, cnt_ref,
                  p_s, xr_s, a_s, u_s, hs_s, hc_s, y_s, yb_s, run_s, hnb_s,
                  *, rg, gm, xa, chunk, n_t):
    tt, d = x_ref.shape
    c1, c2, c3, c4 = rg, 2 * rg, 2 * rg + gm, 2 * rg + 2 * gm
    step = pl.program_id(0)

    @pl.when(step == 0)
    def _():
        run_s[...] = jnp.zeros(run_s.shape, F32)
        hnb_s[...] = jnp.zeros(hnb_s.shape, BF16)

    @pl.when(lax.rem(step, n_t) == 0)
    def _():
        xr_s[0:SUBLANES, :] = jnp.zeros((SUBLANES, rg), F32)
        hc_s[...] = jnp.zeros((SUBLANES, rg), F32)

    _route_previous_tile(step, hnb_s, wrt_ref, br_ref, run_s, idx_ref, gate_ref, rank_ref, cnt_ref)

    x = x_ref[...]
    xn = _rms(x, lnmix_ref[...]).astype(BF16)
    p_s[...] = jnp.dot(xn, win_ref[...], preferred_element_type=F32)

    xr_s[SUBLANES:SUBLANES + tt, :] = p_s[:, 0:c1]
    cw = cw_ref[...]
    kw = cw.shape[0]
    xc = cb_ref[...] + cw[0:1, :] * xr_s[pl.ds(SUBLANES - kw + 1, tt), :]
    for w in range(1, kw):
        xc = xc + cw[w:w + 1, :] * xr_s[pl.ds(SUBLANES - kw + 1 + w, tt), :]
    xr_s[0:SUBLANES, :] = xr_s[tt:tt + SUBLANES, :]

    z = -lam_ref[...]
    sp = jnp.maximum(z, 0.0) + jnp.log1p(jnp.exp(-jnp.abs(z)))
    gates = jnp.dot(xc.astype(BF16), wai_ref[...], preferred_element_type=F32)
    for hd in range(rg // HEAD_DIM):
        sl = slice(hd * HEAD_DIM, (hd + 1) * HEAD_DIM)
        r = _sigmoid(gates[:, hd * HEAD_DIM:(hd + 1) * HEAD_DIM] + ba_ref[:, sl])
        ig = _sigmoid(gates[:, rg + hd * HEAD_DIM:rg + (hd + 1) * HEAD_DIM] + bi_ref[:, sl])
        log_a = (-RG_C) * r * sp[:, sl]
        a = jnp.exp(log_a)
        a_s[:, sl] = a
        u_s[:, sl] = jnp.sqrt(-jnp.tanh(log_a) * (a * a + 1.0)) * (ig * xc[:, sl])

    row = lax.broadcasted_iota(I32, (SUBLANES, rg), 0)
    hprev = hc_s[...]
    for g in range(tt // SUBLANES):
        rows = slice(g * SUBLANES, (g + 1) * SUBLANES)
        aa = a_s[rows, :]
        bb = u_s[rows, :]
        for s in (1, 2, 4):
            keep = row >= s
            bb = jnp.where(keep, aa * pltpu.roll(bb, s, 0) + bb, bb)
            aa = jnp.where(keep, aa * pltpu.roll(aa, s, 0), aa)
        h8 = aa * hprev + bb
        hs_s[rows, :] = h8
        hprev = jnp.broadcast_to(h8[SUBLANES - 1:SUBLANES, :], (SUBLANES, rg))
    hc_s[...] = hprev
    y_s[:, 0:rg] = _gelu(p_s[:, c1:c2]) * hs_s[...]

    vn = _rms(_gelu(p_s[:, c3:c4]), gvn_ref[...]).astype(BF16)
    tri = (lax.broadcasted_iota(I32, (chunk, chunk), 0) >= lax.broadcasted_iota(I32, (chunk, chunk), 1))
    for g in range(gm // HEAD_DIM):
        wg = jnp.where(tri, wsp_ref[g], 0.0).astype(BF16)
        bcol = bst_ref[:, g:g + 1]
        for c in range(tt // chunk):
            rs = slice(c * chunk, (c + 1) * chunk)
            cs = slice(g * HEAD_DIM, (g + 1) * HEAD_DIM)
            sv = jnp.dot(wg, vn[rs, cs], preferred_element_type=F32) + bcol
            y_s[rs, rg + g * HEAD_DIM:rg + (g + 1) * HEAD_DIM] = (
                _gelu(p_s[rs, c2 + g * HEAD_DIM:c2 + (g + 1) * HEAD_DIM]) * sv)

    scale = HEAD_DIM ** -0.5
    for hd in range(xa // HEAD_DIM):
        sl = slice(hd * HEAD_DIM, (hd + 1) * HEAD_DIM)
        q = p_s[:, c4 + hd * HEAD_DIM:c4 + (hd + 1) * HEAD_DIM].astype(BF16)
        s = lax.dot_general(q, k_ref[0, :, sl], (((1,), (1,)), ((), ())),
                            preferred_element_type=F32) * scale
        e = jnp.exp(s - jnp.max(s, axis=-1, keepdims=True))
        o = jnp.dot(e.astype(BF16), v_ref[0, :, sl], preferred_element_type=F32)
        y_s[:, rg + gm + hd * HEAD_DIM:rg + gm + (hd + 1) * HEAD_DIM] = o / jnp.sum(e, axis=-1, keepdims=True)

    for lo, hi in ((0, rg), (rg, rg + gm), (rg + gm, rg + gm + xa)):
        yb_s[:, lo:hi] = _rms(y_s[:, lo:hi], outg_ref[:, lo:hi]).astype(BF16)
    h = x + jnp.dot(yb_s[...], wout_ref[...], preferred_element_type=F32)
    h_ref[...] = h

    hn = _rms(h, lnffn_ref[...])
    _store_packed_rows(hnp_ref, hn, d // (2 * LANES))
    hnb_s[...] = hn.astype(BF16)


def _mixer(x, k, v, prm, *, tt):
    b, t, d = x.shape
    n_t = t // tt
    n = b * t
    rg = prm["conv_w"].shape[1]
    gm = prm["gvn"].shape[1]
    m = k.shape[1]
    xa = k.shape[2]
    chunk = prm["wsp"].shape[1]
    in_cols = prm["w_in"].shape[1]
    n_exp = prm["wrt"].shape[0]
    s_rows = d // (2 * LANES)
    const_names = ("lnmix", "w_in", "conv_w", "conv_b", "wai", "ba", "bi", "lam", "gvn", "wsp", "bst")
    const_names2 = ("outg", "w_out", "lnffn", "wrt", "br")
    total = b * n_t
    tile_of = lambda s: jnp.minimum(s, total - 1)
    in_specs = ([pl.BlockSpec((tt, d), lambda s: (tile_of(s), 0))]
                + [_const_spec(prm[c].shape) for c in const_names]
                + [pl.BlockSpec((1, m, xa), lambda s: (tile_of(s) // n_t, 0, 0)),
                   pl.BlockSpec((1, m, xa), lambda s: (tile_of(s) // n_t, 0, 0))]
                + [_const_spec(prm[c].shape) for c in const_names2])
    routed = lambda s: (0, jnp.maximum(s - 1, 0))
    out_specs = [
        pl.BlockSpec((tt, d), lambda s: (s, 0)),
        pl.BlockSpec((tt * s_rows, LANES), lambda s: (s, 0)),
        pl.BlockSpec((TOP_K, tt), routed),
        pl.BlockSpec((TOP_K, tt), routed),
        pl.BlockSpec((TOP_K, tt), routed),
        pl.BlockSpec((n_exp, LANES), lambda s: (0, 0)),
    ]
    out_shape = [
        jax.ShapeDtypeStruct((n + tt, d), F32),
        jax.ShapeDtypeStruct(((n + tt) * s_rows, LANES), U32),
        jax.ShapeDtypeStruct((TOP_K, n), I32),
        jax.ShapeDtypeStruct((TOP_K, n), F32),
        jax.ShapeDtypeStruct((TOP_K, n), I32),
        jax.ShapeDtypeStruct((n_exp, LANES), F32),
    ]
    scratch = [
        pltpu.VMEM((tt, in_cols), F32),
        pltpu.VMEM((tt + 2 * SUBLANES, rg), F32),
        pltpu.VMEM((tt, rg), F32),
        pltpu.VMEM((tt, rg), F32),
        pltpu.VMEM((tt, rg), F32),
        pltpu.VMEM((SUBLANES, rg), F32),
        pltpu.VMEM((tt, rg + gm + xa), F32),
        pltpu.VMEM((tt, rg + gm + xa), BF16),
        pltpu.VMEM((n_exp, LANES), F32),
        pltpu.VMEM((tt, d), BF16),
    ]
    return pl.pallas_call(
        functools.partial(_mixer_kernel, rg=rg, gm=gm, xa=xa, chunk=chunk, n_t=n_t),
        grid=(total + 1,),
        in_specs=in_specs,
        out_specs=out_specs,
        out_shape=out_shape,
        scratch_shapes=scratch,
        compiler_params=pltpu.CompilerParams(
            dimension_semantics=("arbitrary",), vmem_limit_bytes=VMEM_LIMIT_BYTES),
        name="mixer",
    )(x.reshape(n, d), *[prm[c] for c in const_names], k, v, *[prm[c] for c in const_names2])


def _dispatch_kernel(zs_ref, zl_ref, na_ref, dest_ref, hnp_ref, xs_ref, zb, sem, zsem, *, tile, lag, tm, s_rows):
    step = pl.program_id(0)
    n_exp = zs_ref.shape[0]
    nb = xs_ref.shape[0] // (tm * s_rows)

    def for_each_zero_copy(act):
        def per_expert(e, carry):
            off = zs_ref[e]
            length = zl_ref[e]
            p = tm // 2
            while p >= 1:
                bit = length & p

                @pl.when(bit != 0)
                def _(off=off, p=p):
                    dst = xs_ref.at[pl.ds(pl.multiple_of(off * s_rows, s_rows), p * s_rows)]
                    act(pltpu.make_async_copy(zb.at[pl.ds(0, p * s_rows)], dst, zsem))
                off = off + bit
                p //= 2
            return carry
        lax.fori_loop(0, n_exp, per_expert, 0)

        def per_block(b, carry):
            dst = xs_ref.at[pl.ds(pl.multiple_of(b * (tm * s_rows), tm * s_rows), tm * s_rows)]
            act(pltpu.make_async_copy(zb, dst, zsem))
            return carry
        lax.fori_loop(na_ref[0], nb, per_block, 0)

    @pl.when(step == 0)
    def _():
        zb[...] = jnp.zeros(zb.shape, U32)
        for_each_zero_copy(lambda c: c.start())

    def row_copy(n, dst_row):
        src = hnp_ref.at[pl.ds(pl.multiple_of(n * s_rows, s_rows), s_rows)]
        return pltpu.make_async_copy(src, xs_ref.at[pl.ds(pl.multiple_of(dst_row, s_rows), s_rows)], sem)

    def wait_token():
        for _ in range(TOP_K):
            row_copy(0, 0).wait()

    def body(n, carry):
        for k in range(TOP_K):
            row_copy(n, dest_ref[k * tile + n]).start(priority=k % 2)

        @pl.when(n >= lag)
        def _():
            wait_token()
        return carry

    lax.fori_loop(0, tile, body, 0)

    def drain(_, carry):
        wait_token()
        return carry

    lax.fori_loop(0, lag, drain, 0)

    @pl.when(step == 0)
    def _():
        for_each_zero_copy(lambda c: c.wait())


def _dispatch(zero_start, zero_len, n_active, dest_rows_tiles, hnp, rows, *, tile, tm, s_rows):
    n = dest_rows_tiles.shape[0] // TOP_K
    lag = min(DISPATCH_LAG, tile)
    grid_spec = pltpu.PrefetchScalarGridSpec(
        num_scalar_prefetch=3,
        grid=(n // tile,),
        in_specs=[
            pl.BlockSpec((TOP_K * tile,), lambda i, *_: (i,), memory_space=pltpu.SMEM),
            pl.BlockSpec((tile * s_rows, LANES), lambda i, *_: (i, 0)),
        ],
        out_specs=pl.BlockSpec(memory_space=pl.ANY),
        scratch_shapes=[pltpu.VMEM((tm * s_rows, LANES), U32), pltpu.SemaphoreType.DMA(()),
                        pltpu.SemaphoreType.DMA(())],
    )
    return pl.pallas_call(
        functools.partial(_dispatch_kernel, tile=tile, lag=lag, tm=tm, s_rows=s_rows),
        grid_spec=grid_spec,
        out_shape=jax.ShapeDtypeStruct((rows * s_rows, LANES), U32),
        compiler_params=pltpu.CompilerParams(
            dimension_semantics=("arbitrary",), vmem_limit_bytes=VMEM_LIMIT_BYTES,
            disable_bounds_checks=True, has_side_effects=True),
        name="dispatch",
    )(zero_start, zero_len, n_active, dest_rows_tiles, hnp)


def _expert_changed(be_ref, i):
    prev = be_ref[jnp.maximum(i - 1, 0)]
    return (i == 0) | (be_ref[i] != prev)


def _weight_ring_step(w_hbm, wbuf, wsem, be_ref, nxt_ref, rid_ref, nr_ref, i, j, n_j):
    col_tile = wbuf.shape[2]
    run = j * nr_ref[0] + rid_ref[i]
    slot = run & 1

    def tile_copy(e, jj, sl):
        cols = pl.ds(pl.multiple_of(jj * col_tile, col_tile), col_tile)
        return pltpu.make_async_copy(w_hbm.at[e, :, cols], wbuf.at[sl], wsem.at[sl])

    @pl.when(run == 0)
    def _():
        tile_copy(be_ref[0], 0, 0).start()

    tile_copy(be_ref[i], j, slot).wait()
    nxt = nxt_ref[i]

    @pl.when(nxt >= 0)
    def _():
        tile_copy(nxt, j, 1 - slot).start()

    @pl.when((nxt < 0) & (j + 1 < n_j))
    def _():
        tile_copy(be_ref[0], j + 1, 1 - slot).start()

    return slot


def _moe_up_kernel(be_ref, na_ref, nxt_ref, rid_ref, nr_ref, xs_ref, w_hbm, bg_ref, bu_ref, perm_ref, act_ref,
                   wbuf, wsem, wp_s, xb_s):
    j = pl.program_id(0)
    i = pl.program_id(1)
    tn = wbuf.shape[2]
    hn_ = tn // 2
    d = xb_s.shape[1]
    active = i < na_ref[0]

    @pl.when(active & _expert_changed(be_ref, i))
    def _():
        slot = _weight_ring_step(w_hbm, wbuf, wsem, be_ref, nxt_ref, rid_ref, nr_ref, i, j, pl.num_programs(0))
        for g in range(tn // MXU_DIM):
            wg = wbuf[slot, :, g * MXU_DIM:(g + 1) * MXU_DIM].astype(BF16)
            wq = jnp.dot(wg, perm_ref[...], preferred_element_type=F32).astype(BF16)
            hw = MXU_DIM // 2
            wp_s[:, g * hw:(g + 1) * hw] = wq[:, 0:hw]
            wp_s[:, hn_ + g * hw:hn_ + (g + 1) * hw] = wq[:, hw:MXU_DIM]

    @pl.when(active)
    def _():
        half = d // 2
        s_rows = half // LANES
        tm = xb_s.shape[0]
        rc = min(ROW_CHUNK, tm)
        for r0 in range(0, tm, rc):
            rows = slice(r0, r0 + rc)
            for c in range(s_rows):
                lo, hi = _load_packed_rows(xs_ref, c, rc, s_rows, row0=r0)
                xb_s[rows, c * LANES:(c + 1) * LANES] = lo.astype(BF16)
                xb_s[rows, half + c * LANES:half + (c + 1) * LANES] = hi.astype(BF16)
            gu = jnp.dot(xb_s[rows, :], wp_s[...], preferred_element_type=F32)
            gate = jnp.minimum(gu[:, 0:hn_] + bg_ref[0], SWIGLU_LIMIT)
            up = jnp.clip(gu[:, hn_:tn] + bu_ref[0], -SWIGLU_LIMIT, SWIGLU_LIMIT)
            act_ref[rows, :] = ((up + 1.0) * (gate * _sigmoid(SWIGLU_ALPHA * gate))).astype(BF16)

    @pl.when(jnp.logical_not(active))
    def _():
        act_ref[...] = jnp.zeros(act_ref.shape, BF16)


def _moe_up(plan, xs, w_gate_up, bg, bu, perm, *, tm, tn):
    n_exp, d, f2 = w_gate_up.shape
    s_rows = d // (2 * LANES)
    rows = xs.shape[0] // s_rows
    nb = rows // tm

    def blk(i, na):
        return jnp.minimum(i, na[0] - 1)

    grid_spec = pltpu.PrefetchScalarGridSpec(
        num_scalar_prefetch=len(plan),
        grid=(f2 // tn, nb),
        in_specs=[
            pl.BlockSpec((tm * s_rows, LANES), lambda j, i, be, na, *_: (blk(i, na), 0)),
            pl.BlockSpec(memory_space=pl.ANY),
            pl.BlockSpec((1, 1, tn // 2), lambda j, i, be, na, *_: (be[blk(i, na)], 0, j)),
            pl.BlockSpec((1, 1, tn // 2), lambda j, i, be, na, *_: (be[blk(i, na)], 0, j)),
            pl.BlockSpec((MXU_DIM, MXU_DIM), lambda j, i, *_: (0, 0)),
        ],
        out_specs=pl.BlockSpec((tm, tn // 2), lambda j, i, *_: (i, j)),
        scratch_shapes=[pltpu.VMEM((2, d, tn), F32), pltpu.SemaphoreType.DMA((2,)),
                        pltpu.VMEM((d, tn), BF16), pltpu.VMEM((tm, d), BF16)],
    )
    return pl.pallas_call(
        _moe_up_kernel,
        grid_spec=grid_spec,
        out_shape=jax.ShapeDtypeStruct((rows, f2 // 2), BF16),
        compiler_params=pltpu.CompilerParams(
            dimension_semantics=("arbitrary", "arbitrary"), vmem_limit_bytes=VMEM_LIMIT_BYTES),
        name="moe_up",
    )(*plan, xs, w_gate_up, bg, bu, perm)


def _moe_down_kernel(be_ref, na_ref, nxt_ref, rid_ref, nr_ref, act_ref, w_hbm, b_ref, ys_ref, wbuf, wsem, wb_s):
    i = pl.program_id(0)
    active = i < na_ref[0]

    @pl.when(active & _expert_changed(be_ref, i))
    def _():
        slot = _weight_ring_step(w_hbm, wbuf, wsem, be_ref, nxt_ref, rid_ref, nr_ref, i, 0, 1)
        wb_s[...] = wbuf[slot].astype(BF16)

    @pl.when(active)
    def _():
        n_rows = act_ref.shape[0]
        rc = min(ROW_CHUNK, n_rows)
        for r0 in range(0, n_rows, rc):
            y = jnp.dot(act_ref[r0:r0 + rc, :], wb_s[...], preferred_element_type=F32) + b_ref[0]
            _store_packed_rows(ys_ref, y, y.shape[1] // (2 * LANES), row0=r0)

    @pl.when(jnp.logical_not(active))
    def _():
        ys_ref[...] = jnp.zeros(ys_ref.shape, U32)


def _moe_down(plan, act, w_down, b_down, *, tm):
    rows, f = act.shape
    n_exp, _, d = w_down.shape
    s_rows = d // (2 * LANES)

    def blk(i, na):
        return jnp.minimum(i, na[0] - 1)

    grid_spec = pltpu.PrefetchScalarGridSpec(
        num_scalar_prefetch=len(plan),
        grid=(rows // tm,),
        in_specs=[
            pl.BlockSpec((tm, f), lambda i, be, na, *_: (blk(i, na), 0)),
            pl.BlockSpec(memory_space=pl.ANY),
            pl.BlockSpec((1, 1, d), lambda i, be, na, *_: (be[blk(i, na)], 0, 0)),
        ],
        out_specs=pl.BlockSpec((tm * s_rows, LANES), lambda i, *_: (i, 0)),
        scratch_shapes=[pltpu.VMEM((2, f, d), F32), pltpu.SemaphoreType.DMA((2,)), pltpu.VMEM((f, d), BF16)],
    )
    return pl.pallas_call(
        _moe_down_kernel,
        grid_spec=grid_spec,
        out_shape=jax.ShapeDtypeStruct((rows * s_rows, LANES), U32),
        compiler_params=pltpu.CompilerParams(
            dimension_semantics=("arbitrary",), vmem_limit_bytes=VMEM_LIMIT_BYTES),
        name="moe_down",
    )(*plan, act, w_down, b_down)


def _combine_kernel(dcur_ref, dnext_ref, ys_ref, h_ref, gates_ref, fg_ref, out_ref, buf, sem, *, tile, s_rows):
    s = pl.program_id(0)
    n_steps = pl.num_programs(0)

    slot_rows = TOP_K * tile * s_rows

    def row_copy(slot, k, n, src_row):
        src = ys_ref.at[pl.ds(pl.multiple_of(src_row, s_rows), s_rows)]
        dst = buf.at[slot, pl.ds(pl.multiple_of((k * tile + n) * s_rows, s_rows), s_rows)]
        return pltpu.make_async_copy(src, dst, sem.at[slot])

    def issue(d_ref, slot):
        def body(n, carry):
            for k in range(TOP_K):
                row_copy(slot, k, n, d_ref[k * tile + n]).start(priority=k % 2)
            return carry
        lax.fori_loop(0, tile, body, 0, unroll=ISSUE_UNROLL)

    @pl.when(s == 0)
    def _():
        issue(dcur_ref, 0)

    @pl.when(s + 1 < n_steps)
    def _():
        issue(dnext_ref, (s + 1) % 2)

    def consume(slot):
        pltpu.make_async_copy(ys_ref.at[pl.ds(0, slot_rows)], buf.at[slot], sem.at[slot]).wait()
        gs = [gates_ref[:, k:k + 1] for k in range(TOP_K)]
        d = out_ref.shape[1]
        half = d // 2
        ssq = jnp.zeros((tile, 1), F32)
        for c in range(s_rows):
            lo_cols = slice(c * LANES, (c + 1) * LANES)
            hi_cols = slice(half + c * LANES, half + (c + 1) * LANES)
            acc_lo = h_ref[:, lo_cols]
            acc_hi = h_ref[:, hi_cols]
            for k in range(TOP_K):
                lo, hi = _load_packed_rows(buf.at[slot], c, tile, s_rows, row0=k * tile)
                acc_lo = acc_lo + gs[k] * lo
                acc_hi = acc_hi + gs[k] * hi
            out_ref[:, lo_cols] = acc_lo
            out_ref[:, hi_cols] = acc_hi
            ssq = ssq + jnp.sum(acc_lo * acc_lo + acc_hi * acc_hi, axis=-1, keepdims=True)
        out_ref[...] = out_ref[...] * lax.rsqrt(ssq / d + EPS) * fg_ref[...]

    for slot in range(2):
        @pl.when(s % 2 == slot)
        def _(slot=slot):
            consume(slot)


def _combine(src_rows_tiles, ys, h2, gates_t, fg, *, n, tile):
    d = h2.shape[1]
    s_rows = d // (2 * LANES)
    n_steps = n // tile
    return pl.pallas_call(
        functools.partial(_combine_kernel, tile=tile, s_rows=s_rows),
        grid=(n_steps,),
        in_specs=[
            pl.BlockSpec((TOP_K * tile,), lambda i: (i,), memory_space=pltpu.SMEM),
            pl.BlockSpec((TOP_K * tile,), lambda i: (jnp.minimum(i + 1, n_steps - 1),), memory_space=pltpu.SMEM),
            pl.BlockSpec(memory_space=pl.ANY),
            pl.BlockSpec((tile, d), lambda i: (i, 0)),
            pl.BlockSpec((tile, TOP_K), lambda i: (i, 0)),
            _const_spec((1, d)),
        ],
        out_specs=pl.BlockSpec((tile, d), lambda i: (i, 0)),
        out_shape=jax.ShapeDtypeStruct((n, d), F32),
        scratch_shapes=[pltpu.VMEM((2, TOP_K * tile * s_rows, LANES), U32), pltpu.SemaphoreType.DMA((2,))],
        compiler_params=pltpu.CompilerParams(
            dimension_semantics=("arbitrary",), vmem_limit_bytes=VMEM_LIMIT_BYTES,
            disable_bounds_checks=True),
        name="combine",
    )(src_rows_tiles, src_rows_tiles, ys, h2, gates_t, fg)


def _tile_major(a, tile):
    k, n = a.shape
    return a.reshape(k, n // tile, tile).transpose(1, 0, 2).reshape(-1)


def _layer(h, mem, ln_mix_g, w_in, conv_w, conv_b, w_rg_a, b_rg_a, w_rg_i, b_rg_i, lru_lambda, gm_v_norm_g,
           w_spatial, b_spatial, mem_norm_g, w_mem_k, w_mem_v, out_norm_g, w_out, ln_ffn_g, w_router, b_router,
           w_gate_up, b_gate_up, w_down, b_down, final_g, *, mix_tile, moe_tile, up_tn,
           dispatch_tile, combine_tile):
    b, t, d = h.shape
    n = b * t
    n_exp = w_router.shape[1]
    row = lambda a: a.reshape(1, -1)
    km, v = _mem_kv(mem, row(mem_norm_g), w_mem_k.astype(BF16), w_mem_v.astype(BF16))
    heads = w_rg_a.shape[0]
    eye = jnp.eye(heads, dtype=w_rg_a.dtype)

    def block_diag(w):
        return (eye[:, None, :, None] * w[:, :, None, :]).reshape(heads * w.shape[1], heads * w.shape[2])

    prm = dict(
        lnmix=row(ln_mix_g), w_in=w_in.astype(BF16), conv_w=conv_w, conv_b=row(conv_b),
        wai=jnp.concatenate([block_diag(w_rg_a), block_diag(w_rg_i)], axis=1).astype(BF16),
        ba=row(b_rg_a), bi=row(b_rg_i), lam=row(lru_lambda),
        gvn=row(gm_v_norm_g), wsp=w_spatial, bst=b_spatial.T, outg=row(out_norm_g), w_out=w_out.astype(BF16),
        lnffn=row(ln_ffn_g), wrt=w_router.T.astype(BF16), br=b_router.reshape(-1, 1))
    h1, hnp, idx, gates, rank, cnt = _mixer(h, km, v, prm, tt=mix_tile)

    counts = cnt[:, 0].astype(I32)
    padded = ((counts + moe_tile - 1) // moe_tile) * moe_tile
    pad_end = jnp.cumsum(padded)
    pad_start = pad_end - padded
    s_rows = d // (2 * LANES)
    experts = jnp.arange(n_exp, dtype=I32)
    start_of = jnp.sum(jnp.where(idx[..., None] == experts, pad_start, 0), axis=-1)
    dest_rows = (start_of + rank) * s_rows
    nb = (n * TOP_K) // moe_tile + n_exp
    n_active = (pad_end[-1] // moe_tile).astype(I32).reshape(1)
    block_first = jnp.arange(nb, dtype=I32) * moe_tile
    block_expert = jnp.minimum(
        jnp.sum((pad_end[None, :] <= block_first[:, None]).astype(I32), axis=-1), n_exp - 1)
    blocks = jnp.arange(nb, dtype=I32)
    starts_run = (blocks < n_active[0]) & ((blocks == 0) | (block_expert != jnp.roll(block_expert, 1)))
    run_id = jnp.cumsum(starts_run.astype(I32)) - 1
    later_start = lax.cummin(jnp.where(starts_run, blocks, nb)[::-1])[::-1]
    next_start = jnp.concatenate([later_start[1:], jnp.full((1,), nb, I32)])
    next_expert = jnp.where(next_start < nb, block_expert[jnp.minimum(next_start, nb - 1)], -1).astype(I32)
    n_runs = jnp.sum(starts_run.astype(I32)).reshape(1)
    plan = (block_expert, n_active, next_expert, run_id.astype(I32), n_runs)

    xs = _dispatch(pad_start + counts, padded - counts, n_active, _tile_major(dest_rows, dispatch_tile), hnp,
                   nb * moe_tile, tile=dispatch_tile, tm=moe_tile, s_rows=s_rows)
    f = w_down.shape[1]
    r_ = jnp.arange(MXU_DIM)
    src = jnp.where(r_ < MXU_DIM // 2, 2 * r_, 2 * (r_ - MXU_DIM // 2) + 1)
    perm = (jnp.arange(MXU_DIM)[:, None] == src[None, :]).astype(BF16)
    bg = b_gate_up[:, 0::2].reshape(n_exp, 1, f)
    bu = b_gate_up[:, 1::2].reshape(n_exp, 1, f)
    act = _moe_up(plan, xs, w_gate_up, bg, bu, perm, tm=moe_tile, tn=up_tn)
    ys = _moe_down(plan, act, w_down, b_down.reshape(n_exp, 1, d), tm=moe_tile)
    out = _combine(_tile_major(dest_rows, combine_tile), ys, h1, gates.T, row(final_g), n=n, tile=combine_tile)
    return out.reshape(b, t, d)


def kernel(x, mem, ln_mix_g, w_in, conv_w, conv_b, w_rg_a, b_rg_a, w_rg_i, b_rg_i, lru_lambda, gm_v_norm_g, w_spatial, b_spatial, mem_norm_g, w_mem_k, w_mem_v, out_norm_g, w_out, ln_ffn_g, w_router, b_router, w_gate_up, b_gate_up, w_down, b_down, final_norm_g):
    depth = w_in.shape[0]
    assert depth == 1, "the final RMSNorm is fused into the single layer's combine stage"
    return _layer(
        x, mem, ln_mix_g[0], w_in[0], conv_w[0], conv_b[0], w_rg_a[0], b_rg_a[0], w_rg_i[0], b_rg_i[0],
        lru_lambda[0], gm_v_norm_g[0], w_spatial[0], b_spatial[0], mem_norm_g[0], w_mem_k[0], w_mem_v[0],
        out_norm_g[0], w_out[0], ln_ffn_g[0], w_router[0], b_router[0], w_gate_up[0], b_gate_up[0], w_down[0],
        b_down[0], final_norm_g,
        mix_tile=MIX_TILE, moe_tile=MOE_TILE, up_tn=UP_TN,
        dispatch_tile=DISPATCH_TILE, combine_tile=COMBINE_TILE)
```

```python
import functools

import jax
import jax.numpy as jnp
from jax import lax
from jax.experimental import pallas as pl
from jax.experimental.pallas import tpu as pltpu

F32 = jnp.float32
BF16 = jnp.bfloat16
I32 = jnp.int32
U32 = jnp.uint32

HEAD_DIM = 128
RG_C = 8.0
TOP_K = 4
SWIGLU_LIMIT = 7.0
SWIGLU_ALPHA = 1.702
EPS = 1e-6

LANES = 128
SUBLANES = 8
MXU_DIM = 256
VMEM_LIMIT_BYTES = 56 * 1024 * 1024

MIX_TILE = 256
MOE_TILE = 512
UP_TN = 2048
DISPATCH_TILE = 512
DISPATCH_LAG = 64
COMBINE_TILE = 256
ISSUE_UNROLL = 4


def _rms(x, g):
    return x * lax.rsqrt(jnp.mean(x * x, axis=-1, keepdims=True) + EPS) * g


def _gelu(x):
    return x * (0.5 * (1.0 + jnp.tanh(0.7978845608028654 * (x + 0.044715 * (x * x * x)))))


def _sigmoid(x):
    return 1.0 / (1.0 + jnp.exp(-x))


def _const_spec(shape):
    nd = len(shape)
    return pl.BlockSpec(shape, lambda *_: (0,) * nd, pipeline_mode=pl.Buffered(1))


def _store_packed_rows(ref, y, s_rows):
    rows, d = y.shape
    half = d // 2
    for c in range(s_rows):
        lo = y[:, c * LANES:(c + 1) * LANES].astype(BF16).astype(F32)
        hi = y[:, half + c * LANES:half + (c + 1) * LANES].astype(BF16).astype(F32)
        word = (pltpu.bitcast(lo, U32) >> 16) | (pltpu.bitcast(hi, U32) & jnp.uint32(0xFFFF0000))
        ref[pl.ds(c, rows, stride=s_rows), :] = word


def _load_packed_rows(ref, c, rows, s_rows, row0=0):
    word = ref[pl.ds(row0 * s_rows + c, rows, stride=s_rows), :]
    return pltpu.bitcast(word << 16, F32), pltpu.bitcast(word & jnp.uint32(0xFFFF0000), F32)


def _mem_kv_kernel(mem_ref, g_ref, wk_ref, wv_ref, k_ref, v_ref):
    mn = _rms(mem_ref[0], g_ref[...]).astype(BF16)
    k_ref[0] = jnp.dot(mn, wk_ref[...], preferred_element_type=F32).astype(BF16)
    v_ref[0] = jnp.dot(mn, wv_ref[...], preferred_element_type=F32).astype(BF16)


def _mem_kv(mem, g, wk, wv):
    b, m, d = mem.shape
    xa = wk.shape[1]
    return pl.pallas_call(
        _mem_kv_kernel,
        grid=(b,),
        in_specs=[
            pl.BlockSpec((1, m, d), lambda i: (i, 0, 0)),
            _const_spec((1, d)),
            _const_spec((d, xa)),
            _const_spec((d, xa)),
        ],
        out_specs=[
            pl.BlockSpec((1, m, xa), lambda i: (i, 0, 0)),
            pl.BlockSpec((1, m, xa), lambda i: (i, 0, 0)),
        ],
        out_shape=[jax.ShapeDtypeStruct((b, m, xa), BF16), jax.ShapeDtypeStruct((b, m, xa), BF16)],
        compiler_params=pltpu.CompilerParams(
            dimension_semantics=("arbitrary",), vmem_limit_bytes=VMEM_LIMIT_BYTES),
        name="mem_kv",
    )(mem, g, wk, wv)


def _route_previous_tile(step, hnb_s, wrt_ref, br_ref, run_s, idx_ref, gate_ref, rank_ref, cnt_ref):
    n_exp = wrt_ref.shape[0]
    tt = hnb_s.shape[0]
    lg = lax.dot_general(wrt_ref[...], hnb_s[...], (((1,), (1,)), ((), ())),
                         preferred_element_type=F32) + br_ref[...]
    eid = lax.broadcasted_iota(I32, (n_exp, tt), 0)
    vals, sels = [], []
    for k in range(TOP_K):
        m = jnp.max(lg, axis=0, keepdims=True)
        ik = jnp.min(jnp.where(lg == m, eid, n_exp), axis=0, keepdims=True)
        sel = eid == ik
        vals.append(m)
        sels.append(sel)
        idx_ref[k:k + 1, :] = ik
        lg = jnp.where(sel, -jnp.inf, lg)
    es = [jnp.exp(v - vals[0]) for v in vals]
    den = es[0] + es[1] + es[2] + es[3]
    for k in range(TOP_K):
        gate_ref[k:k + 1, :] = es[k] / den
    oh = jnp.where(sels[0] | sels[1] | sels[2] | sels[3], 1.0, 0.0)
    upper = jnp.where(lax.broadcasted_iota(I32, (tt, tt), 0) < lax.broadcasted_iota(I32, (tt, tt), 1),
                      1.0, 0.0).astype(BF16)
    before = jnp.dot(oh.astype(BF16), upper, preferred_element_type=F32) + run_s[:, 0:1]
    for k in range(TOP_K):
        rank_ref[k:k + 1, :] = jnp.sum(jnp.where(sels[k], before, 0.0), axis=0, keepdims=True).astype(I32)
    counted = jnp.where(step >= 1, 1.0, 0.0)
    run_s[...] = run_s[...] + counted * jnp.sum(oh, axis=1, keepdims=True)
    cnt_ref[...] = run_s[...]


def _mixer_kernel(x_ref, lnmix_ref, win_ref, cw_ref, cb_ref, wai_ref, ba_ref, bi_ref, lam_ref,
                  gvn_ref, wsp_ref, bst_ref, k_ref, v_ref, outg_ref, wout_ref, lnffn_ref, wrt_ref, br_ref,
                  h_ref, hnp_ref, idx_ref, gate_ref, rank_ref, cnt_ref,
                  p_s, xr_s, a_s, u_s, hs_s, hc_s, y_s, yb_s, run_s, hnb_s,
                  *, rg, gm, xa, chunk, n_t):
    tt, d = x_ref.shape
    c1, c2, c3, c4 = rg, 2 * rg, 2 * rg + gm, 2 * rg + 2 * gm
    step = pl.program_id(0)

    @pl.when(step == 0)
    def _():
        run_s[...] = jnp.zeros(run_s.shape, F32)
        hnb_s[...] = jnp.zeros(hnb_s.shape, BF16)

    @pl.when(lax.rem(step, n_t) == 0)
    def _():
        xr_s[0:SUBLANES, :] = jnp.zeros((SUBLANES, rg), F32)
        hc_s[...] = jnp.zeros((SUBLANES, rg), F32)

    _route_previous_tile(step, hnb_s, wrt_ref, br_ref, run_s, idx_ref, gate_ref, rank_ref, cnt_ref)

    x = x_ref[...]
    xn = _rms(x, lnmix_ref[...]).astype(BF16)
    p_s[...] = jnp.dot(xn, win_ref[...], preferred_element_type=F32)

    xr_s[SUBLANES:SUBLANES + tt, :] = p_s[:, 0:c1]
    cw = cw_ref[...]
    kw = cw.shape[0]
    xc = cb_ref[...] + cw[0:1, :] * xr_s[pl.ds(SUBLANES - kw + 1, tt), :]
    for w in range(1, kw):
        xc = xc + cw[w:w + 1, :] * xr_s[pl.ds(SUBLANES - kw + 1 + w, tt), :]
    xr_s[0:SUBLANES, :] = xr_s[tt:tt + SUBLANES, :]

    z = -lam_ref[...]
    sp = jnp.maximum(z, 0.0) + jnp.log1p(jnp.exp(-jnp.abs(z)))
    gates = jnp.dot(xc.astype(BF16), wai_ref[...], preferred_element_type=F32)
    for hd in range(rg // HEAD_DIM):
        sl = slice(hd * HEAD_DIM, (hd + 1) * HEAD_DIM)
        r = _sigmoid(gates[:, hd * HEAD_DIM:(hd + 1) * HEAD_DIM] + ba_ref[:, sl])
        ig = _sigmoid(gates[:, rg + hd * HEAD_DIM:rg + (hd + 1) * HEAD_DIM] + bi_ref[:, sl])
        log_a = (-RG_C) * r * sp[:, sl]
        a = jnp.exp(log_a)
        a_s[:, sl] = a
        u_s[:, sl] = jnp.sqrt(-jnp.tanh(log_a) * (a * a + 1.0)) * (ig * xc[:, sl])

    row = lax.broadcasted_iota(I32, (SUBLANES, rg), 0)
    hprev = hc_s[...]
    for g in range(tt // SUBLANES):
        rows = slice(g * SUBLANES, (g + 1) * SUBLANES)
        aa = a_s[rows, :]
        bb = u_s[rows, :]
        for s in (1, 2, 4):
            keep = row >= s
            bb = jnp.where(keep, aa * pltpu.roll(bb, s, 0) + bb, bb)
            aa = jnp.where(keep, aa * pltpu.roll(aa, s, 0), aa)
        h8 = aa * hprev + bb
        hs_s[rows, :] = h8
        hprev = jnp.broadcast_to(h8[SUBLANES - 1:SUBLANES, :], (SUBLANES, rg))
    hc_s[...] = hprev
    y_s[:, 0:rg] = _gelu(p_s[:, c1:c2]) * hs_s[...]

    vn = _rms(_gelu(p_s[:, c3:c4]), gvn_ref[...]).astype(BF16)
    tri = (lax.broadcasted_iota(I32, (chunk, chunk), 0) >= lax.broadcasted_iota(I32, (chunk, chunk), 1))
    for g in range(gm // HEAD_DIM):
        wg = jnp.where(tri, wsp_ref[g], 0.0).astype(BF16)
        bcol = bst_ref[:, g:g + 1]
        for c in range(tt // chunk):
            rs = slice(c * chunk, (c + 1) * chunk)
            cs = slice(g * HEAD_DIM, (g + 1) * HEAD_DIM)
            sv = jnp.dot(wg, vn[rs, cs], preferred_element_type=F32) + bcol
            y_s[rs, rg + g * HEAD_DIM:rg + (g + 1) * HEAD_DIM] = (
                _gelu(p_s[rs, c2 + g * HEAD_DIM:c2 + (g + 1) * HEAD_DIM]) * sv)

    scale = HEAD_DIM ** -0.5
    for hd in range(xa // HEAD_DIM):
        sl = slice(hd * HEAD_DIM, (hd + 1) * HEAD_DIM)
        q = p_s[:, c4 + hd * HEAD_DIM:c4 + (hd + 1) * HEAD_DIM].astype(BF16)
        s = lax.dot_general(q, k_ref[0, :, sl], (((1,), (1,)), ((), ())),
                            preferred_element_type=F32) * scale
        e = jnp.exp(s - jnp.max(s, axis=-1, keepdims=True))
        o = jnp.dot(e.astype(BF16), v_ref[0, :, sl], preferred_element_type=F32)
        y_s[:, rg + gm + hd * HEAD_DIM:rg + gm + (hd + 1) * HEAD_DIM] = o / jnp.sum(e, axis=-1, keepdims=True)

    for lo, hi in ((0, rg), (rg, rg + gm), (rg + gm, rg + gm + xa)):
        yb_s[:, lo:hi] = _rms(y_s[:, lo:hi], outg_ref[:, lo:hi]).astype(BF16)
    h = x + jnp.dot(yb_s[...], wout_ref[...], preferred_element_type=F32)
    h_ref[...] = h

    hn = _rms(h, lnffn_ref[...])
    _store_packed_rows(hnp_ref, hn, d // (2 * LANES))
    hnb_s[...] = hn.astype(BF16)


def _mixer(x, k, v, prm, *, tt):
    b, t, d = x.shape
    n_t = t // tt
    n = b * t
    rg = prm["conv_w"].shape[1]
    gm = prm["gvn"].shape[1]
    m = k.shape[1]
    xa = k.shape[2]
    chunk = prm["wsp"].shape[1]
    in_cols = prm["w_in"].shape[1]
    n_exp = prm["wrt"].shape[0]
    s_rows = d // (2 * LANES)
    const_names = ("lnmix", "w_in", "conv_w", "conv_b", "wai", "ba", "bi", "lam", "gvn", "wsp", "bst")
    const_names2 = ("outg", "w_out", "lnffn", "wrt", "br")
    total = b * n_t
    tile_of = lambda s: jnp.minimum(s, total - 1)
    in_specs = ([pl.BlockSpec((tt, d), lambda s: (tile_of(s), 0))]
                + [_const_spec(prm[c].shape) for c in const_names]
                + [pl.BlockSpec((1, m, xa), lambda s: (tile_of(s) // n_t, 0, 0)),
                   pl.BlockSpec((1, m, xa), lambda s: (tile_of(s) // n_t, 0, 0))]
                + [_const_spec(prm[c].shape) for c in const_names2])
    routed = lambda s: (0, jnp.maximum(s - 1, 0))
    out_specs = [
        pl.BlockSpec((tt, d), lambda s: (s, 0)),
        pl.BlockSpec((tt * s_rows, LANES), lambda s: (s, 0)),
        pl.BlockSpec((TOP_K, tt), routed),
        pl.BlockSpec((TOP_K, tt), routed),
        pl.BlockSpec((TOP_K, tt), routed),
        pl.BlockSpec((n_exp, LANES), lambda s: (0, 0)),
    ]
    out_shape = [
        jax.ShapeDtypeStruct((n + tt, d), F32),
        jax.ShapeDtypeStruct(((n + tt) * s_rows, LANES), U32),
        jax.ShapeDtypeStruct((TOP_K, n), I32),
        jax.ShapeDtypeStruct((TOP_K, n), F32),
        jax.ShapeDtypeStruct((TOP_K, n), I32),
        jax.ShapeDtypeStruct((n_exp, LANES), F32),
    ]
    scratch = [
        pltpu.VMEM((tt, in_cols), F32),
        pltpu.VMEM((tt + 2 * SUBLANES, rg), F32),
        pltpu.VMEM((tt, rg), F32),
        pltpu.VMEM((tt, rg), F32),
        pltpu.VMEM((tt, rg), F32),
        pltpu.VMEM((SUBLANES, rg), F32),
        pltpu.VMEM((tt, rg + gm + xa), F32),
        pltpu.VMEM((tt, rg + gm + xa), BF16),
        pltpu.VMEM((n_exp, LANES), F32),
        pltpu.VMEM((tt, d), BF16),
    ]
    return pl.pallas_call(
        functools.partial(_mixer_kernel, rg=rg, gm=gm, xa=xa, chunk=chunk, n_t=n_t),
        grid=(total + 1,),
        in_specs=in_specs,
        out_specs=out_specs,
        out_shape=out_shape,
        scratch_shapes=scratch,
        compiler_params=pltpu.CompilerParams(
            dimension_semantics=("arbitrary",), vmem_limit_bytes=VMEM_LIMIT_BYTES),
        name="mixer",
    )(x.reshape(n, d), *[prm[c] for c in const_names], k, v, *[prm[c] for c in const_names2])


def _dispatch_kernel(zs_ref, zl_ref, na_ref, dest_ref, hnp_ref, xs_ref, zb, sem, zsem, *, tile, lag, tm, s_rows):
    step = pl.program_id(0)
    n_exp = zs_ref.shape[0]
    nb = xs_ref.shape[0] // (tm * s_rows)

    def for_each_zero_copy(act):
        def per_expert(e, carry):
            off = zs_ref[e]
            length = zl_ref[e]
            p = tm // 2
            while p >= 1:
                bit = length & p

                @pl.when(bit != 0)
                def _(off=off, p=p):
                    dst = xs_ref.at[pl.ds(pl.multiple_of(off * s_rows, s_rows), p * s_rows)]
                    act(pltpu.make_async_copy(zb.at[pl.ds(0, p * s_rows)], dst, zsem))
                off = off + bit
                p //= 2
            return carry
        lax.fori_loop(0, n_exp, per_expert, 0)

        def per_block(b, carry):
            dst = xs_ref.at[pl.ds(pl.multiple_of(b * (tm * s_rows), tm * s_rows), tm * s_rows)]
            act(pltpu.make_async_copy(zb, dst, zsem))
            return carry
        lax.fori_loop(na_ref[0], nb, per_block, 0)

    @pl.when(step == 0)
    def _():
        zb[...] = jnp.zeros(zb.shape, U32)
        for_each_zero_copy(lambda c: c.start())

    def row_copy(n, dst_row):
        src = hnp_ref.at[pl.ds(pl.multiple_of(n * s_rows, s_rows), s_rows)]
        return pltpu.make_async_copy(src, xs_ref.at[pl.ds(pl.multiple_of(dst_row, s_rows), s_rows)], sem)

    def wait_token():
        for _ in range(TOP_K):
            row_copy(0, 0).wait()

    def body(n, carry):
        for k in range(TOP_K):
            row_copy(n, dest_ref[k * tile + n]).start(priority=k % 2)

        @pl.when(n >= lag)
        def _():
            wait_token()
        return carry

    lax.fori_loop(0, tile, body, 0)

    def drain(_, carry):
        wait_token()
        return carry

    lax.fori_loop(0, lag, drain, 0)

    @pl.when(step == 0)
    def _():
        for_each_zero_copy(lambda c: c.wait())


def _dispatch(zero_start, zero_len, n_active, dest_rows_tiles, hnp, rows, *, tile, tm, s_rows):
    n = dest_rows_tiles.shape[0] // TOP_K
    lag = min(DISPATCH_LAG, tile)
    grid_spec = pltpu.PrefetchScalarGridSpec(
        num_scalar_prefetch=3,
        grid=(n // tile,),
        in_specs=[
            pl.BlockSpec((TOP_K * tile,), lambda i, *_: (i,), memory_space=pltpu.SMEM),
            pl.BlockSpec((tile * s_rows, LANES), lambda i, *_: (i, 0)),
        ],
        out_specs=pl.BlockSpec(memory_space=pl.ANY),
        scratch_shapes=[pltpu.VMEM((tm * s_rows, LANES), U32), pltpu.SemaphoreType.DMA(()),
                        pltpu.SemaphoreType.DMA(())],
    )
    return pl.pallas_call(
        functools.partial(_dispatch_kernel, tile=tile, lag=lag, tm=tm, s_rows=s_rows),
        grid_spec=grid_spec,
        out_shape=jax.ShapeDtypeStruct((rows * s_rows, LANES), U32),
        compiler_params=pltpu.CompilerParams(
            dimension_semantics=("arbitrary",), vmem_limit_bytes=VMEM_LIMIT_BYTES,
            disable_bounds_checks=True, has_side_effects=True),
        name="dispatch",
    )(zero_start, zero_len, n_active, dest_rows_tiles, hnp)


def _expert_changed(be_ref, i):
    prev = be_ref[jnp.maximum(i - 1, 0)]
    return (i == 0) | (be_ref[i] != prev)


def _weight_ring_step(w_hbm, wbuf, wsem, be_ref, nxt_ref, rid_ref, nr_ref, i, j, n_j):
    col_tile = wbuf.shape[2]
    run = j * nr_ref[0] + rid_ref[i]
    slot = run & 1

    def tile_copy(e, jj, sl):
        cols = pl.ds(pl.multiple_of(jj * col_tile, col_tile), col_tile)
        return pltpu.make_async_copy(w_hbm.at[e, :, cols], wbuf.at[sl], wsem.at[sl])

    @pl.when(run == 0)
    def _():
        tile_copy(be_ref[0], 0, 0).start()

    tile_copy(be_ref[i], j, slot).wait()
    nxt = nxt_ref[i]

    @pl.when(nxt >= 0)
    def _():
        tile_copy(nxt, j, 1 - slot).start()

    @pl.when((nxt < 0) & (j + 1 < n_j))
    def _():
        tile_copy(be_ref[0], j + 1, 1 - slot).start()

    return slot


def _moe_up_kernel(be_ref, na_ref, nxt_ref, rid_ref, nr_ref, xs_ref, w_hbm, bg_ref, bu_ref, perm_ref, act_ref,
                   wbuf, wsem, wp_s, xb_s):
    j = pl.program_id(0)
    i = pl.program_id(1)
    tn = wbuf.shape[2]
    hn_ = tn // 2
    d = xb_s.shape[1]
    active = i < na_ref[0]

    @pl.when(active & _expert_changed(be_ref, i))
    def _():
        slot = _weight_ring_step(w_hbm, wbuf, wsem, be_ref, nxt_ref, rid_ref, nr_ref, i, j, pl.num_programs(0))
        for g in range(tn // MXU_DIM):
            wg = wbuf[slot, :, g * MXU_DIM:(g + 1) * MXU_DIM].astype(BF16)
            wq = jnp.dot(wg, perm_ref[...], preferred_element_type=F32).astype(BF16)
            hw = MXU_DIM // 2
            wp_s[:, g * hw:(g + 1) * hw] = wq[:, 0:hw]
            wp_s[:, hn_ + g * hw:hn_ + (g + 1) * hw] = wq[:, hw:MXU_DIM]

    @pl.when(active)
    def _():
        half = d // 2
        s_rows = half // LANES
        tm = xb_s.shape[0]
        for c in range(s_rows):
            lo, hi = _load_packed_rows(xs_ref, c, tm, s_rows)
            xb_s[:, c * LANES:(c + 1) * LANES] = lo.astype(BF16)
            xb_s[:, half + c * LANES:half + (c + 1) * LANES] = hi.astype(BF16)
        gu = jnp.dot(xb_s[...], wp_s[...], preferred_element_type=F32)
        gate = jnp.minimum(gu[:, 0:hn_] + bg_ref[0], SWIGLU_LIMIT)
        up = jnp.clip(gu[:, hn_:tn] + bu_ref[0], -SWIGLU_LIMIT, SWIGLU_LIMIT)
        act_ref[...] = ((up + 1.0) * (gate * _sigmoid(SWIGLU_ALPHA * gate))).astype(BF16)

    @pl.when(jnp.logical_not(active))
    def _():
        act_ref[...] = jnp.zeros(act_ref.shape, BF16)


def _moe_up(plan, xs, w_gate_up, bg, bu, perm, *, tm, tn):
    n_exp, d, f2 = w_gate_up.shape
    s_rows = d // (2 * LANES)
    rows = xs.shape[0] // s_rows
    nb = rows // tm

    def blk(i, na):
        return jnp.minimum(i, na[0] - 1)

    grid_spec = pltpu.PrefetchScalarGridSpec(
        num_scalar_prefetch=len(plan),
        grid=(f2 // tn, nb),
        in_specs=[
            pl.BlockSpec((tm * s_rows, LANES), lambda j, i, be, na, *_: (blk(i, na), 0)),
            pl.BlockSpec(memory_space=pl.ANY),
            pl.BlockSpec((1, 1, tn // 2), lambda j, i, be, na, *_: (be[blk(i, na)], 0, j)),
            pl.BlockSpec((1, 1, tn // 2), lambda j, i, be, na, *_: (be[blk(i, na)], 0, j)),
            pl.BlockSpec((MXU_DIM, MXU_DIM), lambda j, i, *_: (0, 0)),
        ],
        out_specs=pl.BlockSpec((tm, tn // 2), lambda j, i, *_: (i, j)),
        scratch_shapes=[pltpu.VMEM((2, d, tn), F32), pltpu.SemaphoreType.DMA((2,)),
                        pltpu.VMEM((d, tn), BF16), pltpu.VMEM((tm, d), BF16)],
    )
    return pl.pallas_call(
        _moe_up_kernel,
        grid_spec=grid_spec,
        out_shape=jax.ShapeDtypeStruct((rows, f2 // 2), BF16),
        compiler_params=pltpu.CompilerParams(
            dimension_semantics=("arbitrary", "arbitrary"), vmem_limit_bytes=VMEM_LIMIT_BYTES),
        name="moe_up",
    )(*plan, xs, w_gate_up, bg, bu, perm)


def _moe_down_kernel(be_ref, na_ref, nxt_ref, rid_ref, nr_ref, act_ref, w_hbm, b_ref, ys_ref, wbuf, wsem, wb_s):
    i = pl.program_id(0)
    active = i < na_ref[0]

    @pl.when(active & _expert_changed(be_ref, i))
    def _():
        slot = _weight_ring_step(w_hbm, wbuf, wsem, be_ref, nxt_ref, rid_ref, nr_ref, i, 0, 1)
        wb_s[...] = wbuf[slot].astype(BF16)

    @pl.when(active)
    def _():
        y = jnp.dot(act_ref[...], wb_s[...], preferred_element_type=F32) + b_ref[0]
        _store_packed_rows(ys_ref, y, y.shape[1] // (2 * LANES))

    @pl.when(jnp.logical_not(active))
    def _():
        ys_ref[...] = jnp.zeros(ys_ref.shape, U32)


def _moe_down(plan, act, w_down, b_down, *, tm):
    rows, f = act.shape
    n_exp, _, d = w_down.shape
    s_rows = d // (2 * LANES)

    def blk(i, na):
        return jnp.minimum(i, na[0] - 1)

    grid_spec = pltpu.PrefetchScalarGridSpec(
        num_scalar_prefetch=len(plan),
        grid=(rows // tm,),
        in_specs=[
            pl.BlockSpec((tm, f), lambda i, be, na, *_: (blk(i, na), 0)),
            pl.BlockSpec(memory_space=pl.ANY),
            pl.BlockSpec((1, 1, d), lambda i, be, na, *_: (be[blk(i, na)], 0, 0)),
        ],
        out_specs=pl.BlockSpec((tm * s_rows, LANES), lambda i, *_: (i, 0)),
        scratch_shapes=[pltpu.VMEM((2, f, d), F32), pltpu.SemaphoreType.DMA((2,)), pltpu.VMEM((f, d), BF16)],
    )
    return pl.pallas_call(
        _moe_down_kernel,
        grid_spec=grid_spec,
        out_shape=jax.ShapeDtypeStruct((rows * s_rows, LANES), U32),
        compiler_params=pltpu.CompilerParams(
            dimension_semantics=("arbitrary",), vmem_limit_bytes=VMEM_LIMIT_BYTES),
        name="moe_down",
    )(*plan, act, w_down, b_down)


def _combine_kernel(dcur_ref, dnext_ref, ys_ref, h_ref, gates_ref, fg_ref, out_ref, buf, sem, *, tile, s_rows):
    s = pl.program_id(0)
    n_steps = pl.num_programs(0)

    slot_rows = TOP_K * tile * s_rows

    def row_copy(slot, k, n, src_row):
        src = ys_ref.at[pl.ds(pl.multiple_of(src_row, s_rows), s_rows)]
        dst = buf.at[slot, pl.ds(pl.multiple_of((k * tile + n) * s_rows, s_rows), s_rows)]
        return pltpu.make_async_copy(src, dst, sem.at[slot])

    def issue(d_ref, slot):
        def body(n, carry):
            for k in range(TOP_K):
                row_copy(slot, k, n, d_ref[k * tile + n]).start(priority=k % 2)
            return carry
        lax.fori_loop(0, tile, body, 0, unroll=ISSUE_UNROLL)

    @pl.when(s == 0)
    def _():
        issue(dcur_ref, 0)

    @pl.when(s + 1 < n_steps)
    def _():
        issue(dnext_ref, (s + 1) % 2)

    def consume(slot):
        pltpu.make_async_copy(ys_ref.at[pl.ds(0, slot_rows)], buf.at[slot], sem.at[slot]).wait()
        gs = [gates_ref[:, k:k + 1] for k in range(TOP_K)]
        d = out_ref.shape[1]
        half = d // 2
        ssq = jnp.zeros((tile, 1), F32)
        for c in range(s_rows):
            lo_cols = slice(c * LANES, (c + 1) * LANES)
            hi_cols = slice(half + c * LANES, half + (c + 1) * LANES)
            acc_lo = h_ref[:, lo_cols]
            acc_hi = h_ref[:, hi_cols]
            for k in range(TOP_K):
                lo, hi = _load_packed_rows(buf.at[slot], c, tile, s_rows, row0=k * tile)
                acc_lo = acc_lo + gs[k] * lo
                acc_hi = acc_hi + gs[k] * hi
            out_ref[:, lo_cols] = acc_lo
            out_ref[:, hi_cols] = acc_hi
            ssq = ssq + jnp.sum(acc_lo * acc_lo + acc_hi * acc_hi, axis=-1, keepdims=True)
        out_ref[...] = out_ref[...] * lax.rsqrt(ssq / d + EPS) * fg_ref[...]

    for slot in range(2):
        @pl.when(s % 2 == slot)
        def _(slot=slot):
            consume(slot)


def _combine(src_rows_tiles, ys, h2, gates_t, fg, *, n, tile):
    d = h2.shape[1]
    s_rows = d // (2 * LANES)
    n_steps = n // tile
    return pl.pallas_call(
        functools.partial(_combine_kernel, tile=tile, s_rows=s_rows),
        grid=(n_steps,),
        in_specs=[
            pl.BlockSpec((TOP_K * tile,), lambda i: (i,), memory_space=pltpu.SMEM),
            pl.BlockSpec((TOP_K * tile,), lambda i: (jnp.minimum(i + 1, n_steps - 1),), memory_space=pltpu.SMEM),
            pl.BlockSpec(memory_space=pl.ANY),
            pl.BlockSpec((tile, d), lambda i: (i, 0)),
            pl.BlockSpec((tile, TOP_K), lambda i: (i, 0)),
            _const_spec((1, d)),
        ],
        out_specs=pl.BlockSpec((tile, d), lambda i: (i, 0)),
        out_shape=jax.ShapeDtypeStruct((n, d), F32),
        scratch_shapes=[pltpu.VMEM((2, TOP_K * tile * s_rows, LANES), U32), pltpu.SemaphoreType.DMA((2,))],
        compiler_params=pltpu.CompilerParams(
            dimension_semantics=("arbitrary",), vmem_limit_bytes=VMEM_LIMIT_BYTES,
            disable_bounds_checks=True),
        name="combine",
    )(src_rows_tiles, src_rows_tiles, ys, h2, gates_t, fg)


def _tile_major(a, tile):
    k, n = a.shape
    return a.reshape(k, n // tile, tile).transpose(1, 0, 2).reshape(-1)


def _layer(h, mem, ln_mix_g, w_in, conv_w, conv_b, w_rg_a, b_rg_a, w_rg_i, b_rg_i, lru_lambda, gm_v_norm_g,
           w_spatial, b_spatial, mem_norm_g, w_mem_k, w_mem_v, out_norm_g, w_out, ln_ffn_g, w_router, b_router,
           w_gate_up, b_gate_up, w_down, b_down, final_g, *, mix_tile, moe_tile, up_tn,
           dispatch_tile, combine_tile):
    b, t, d = h.shape
    n = b * t
    n_exp = w_router.shape[1]
    row = lambda a: a.reshape(1, -1)
    km, v = _mem_kv(mem, row(mem_norm_g), w_mem_k.astype(BF16), w_mem_v.astype(BF16))
    heads = w_rg_a.shape[0]
    eye = jnp.eye(heads, dtype=w_rg_a.dtype)

    def block_diag(w):
        return (eye[:, None, :, None] * w[:, :, None, :]).reshape(heads * w.shape[1], heads * w.shape[2])

    prm = dict(
        lnmix=row(ln_mix_g), w_in=w_in.astype(BF16), conv_w=conv_w, conv_b=row(conv_b),
        wai=jnp.concatenate([block_diag(w_rg_a), block_diag(w_rg_i)], axis=1).astype(BF16),
        ba=row(b_rg_a), bi=row(b_rg_i), lam=row(lru_lambda),
        gvn=row(gm_v_norm_g), wsp=w_spatial, bst=b_spatial.T, outg=row(out_norm_g), w_out=w_out.astype(BF16),
        lnffn=row(ln_ffn_g), wrt=w_router.T.astype(BF16), br=b_router.reshape(-1, 1))
    h1, hnp, idx, gates, rank, cnt = _mixer(h, km, v, prm, tt=mix_tile)

    counts = cnt[:, 0].astype(I32)
    padded = ((counts + moe_tile - 1) // moe_tile) * moe_tile
    pad_end = jnp.cumsum(padded)
    pad_start = pad_end - padded
    s_rows = d // (2 * LANES)
    experts = jnp.arange(n_exp, dtype=I32)
    start_of = jnp.sum(jnp.where(idx[..., None] == experts, pad_start, 0), axis=-1)
    dest_rows = (start_of + rank) * s_rows
    nb = (n * TOP_K) // moe_tile + n_exp
    n_active = (pad_end[-1] // moe_tile).astype(I32).reshape(1)
    block_first = jnp.arange(nb, dtype=I32) * moe_tile
    block_expert = jnp.minimum(
        jnp.sum((pad_end[None, :] <= block_first[:, None]).astype(I32), axis=-1), n_exp - 1)
    blocks = jnp.arange(nb, dtype=I32)
    starts_run = (blocks < n_active[0]) & ((blocks == 0) | (block_expert != jnp.roll(block_expert, 1)))
    run_id = jnp.cumsum(starts_run.astype(I32)) - 1
    later_start = lax.cummin(jnp.where(starts_run, blocks, nb)[::-1])[::-1]
    next_start = jnp.concatenate([later_start[1:], jnp.full((1,), nb, I32)])
    next_expert = jnp.where(next_start < nb, block_expert[jnp.minimum(next_start, nb - 1)], -1).astype(I32)
    n_runs = jnp.sum(starts_run.astype(I32)).reshape(1)
    plan = (block_expert, n_active, next_expert, run_id.astype(I32), n_runs)

    xs = _dispatch(pad_start + counts, padded - counts, n_active, _tile_major(dest_rows, dispatch_tile), hnp,
                   nb * moe_tile, tile=dispatch_tile, tm=moe_tile, s_rows=s_rows)
    f = w_down.shape[1]
    r_ = jnp.arange(MXU_DIM)
    src = jnp.where(r_ < MXU_DIM // 2, 2 * r_, 2 * (r_ - MXU_DIM // 2) + 1)
    perm = (jnp.arange(MXU_DIM)[:, None] == src[None, :]).astype(BF16)
    bg = b_gate_up[:, 0::2].reshape(n_exp, 1, f)
    bu = b_gate_up[:, 1::2].reshape(n_exp, 1, f)
    act = _moe_up(plan, xs, w_gate_up, bg, bu, perm, tm=moe_tile, tn=up_tn)
    ys = _moe_down(plan, act, w_down, b_down.reshape(n_exp, 1, d), tm=moe_tile)
    out = _combine(_tile_major(dest_rows, combine_tile), ys, h1, gates.T, row(final_g), n=n, tile=combine_tile)
    return out.reshape(b, t, d)


def kernel(x, mem, ln_mix_g, w_in, conv_w, conv_b, w_rg_a, b_rg_a, w_rg_i, b_rg_i, lru_lambda, gm_v_norm_g, w_spatial, b_spatial, mem_norm_g, w_mem_k, w_mem_v, out_norm_g, w_out, ln_ffn_g, w_router, b_router, w_gate_up, b_gate_up, w_down, b_down, final_norm_g):
    depth = w_in.shape[0]
    assert depth == 1, "the final RMSNorm is fused into the single layer's combine stage"
    return _layer(
        x, mem, ln_mix_g[0], w_in[0], conv_w[0], conv_b[0], w_rg_a[0], b_rg_a[0], w_rg_i[0], b_rg_i[0],
        lru_lambda[0], gm_v_norm_g[0], w_spatial[0], b_spatial[0], mem_norm_g[0], w_mem_k[0], w_mem_v[0],
        out_norm_g[0], w_out[0], ln_ffn_g[0], w_router[0], b_router[0], w_gate_up[0], b_gate_up[0], w_down[0],
        b_down[0], final_norm_g,
        mix_tile=MIX_TILE, moe_tile=MOE_TILE, up_tn=UP_TN,
        dispatch_tile=DISPATCH_TILE, combine_tile=COMBINE_TILE)
```

```python
import functools

import jax
import jax.numpy as jnp
from jax import lax
from jax.experimental import pallas as pl
from jax.experimental.pallas import tpu as pltpu

F32 = jnp.float32
BF16 = jnp.bfloat16
I32 = jnp.int32
U32 = jnp.uint32

HEAD_DIM = 128
RG_C = 8.0
TOP_K = 4
SWIGLU_LIMIT = 7.0
SWIGLU_ALPHA = 1.702
EPS = 1e-6

LANES = 128
SUBLANES = 8
MXU_DIM = 256
VMEM_LIMIT_BYTES = 56 * 1024 * 1024

MIX_TILE = 256
MOE_TILE = 512
MOE_QUANTUM = 128
UP_TN = 2048
DISPATCH_TILE = 512
DISPATCH_LAG = 64
COMBINE_TILE = 256
ISSUE_UNROLL = 4


def _rms(x, g):
    return x * lax.rsqrt(jnp.mean(x * x, axis=-1, keepdims=True) + EPS) * g


def _gelu(x):
    return x * (0.5 * (1.0 + jnp.tanh(0.7978845608028654 * (x + 0.044715 * (x * x * x)))))


def _sigmoid(x):
    return 1.0 / (1.0 + jnp.exp(-x))


def _const_spec(shape):
    nd = len(shape)
    return pl.BlockSpec(shape, lambda *_: (0,) * nd, pipeline_mode=pl.Buffered(1))


def _store_packed_rows(ref, y, s_rows):
    rows, d = y.shape
    half = d // 2
    for c in range(s_rows):
        lo = y[:, c * LANES:(c + 1) * LANES].astype(BF16).astype(F32)
        hi = y[:, half + c * LANES:half + (c + 1) * LANES].astype(BF16).astype(F32)
        word = (pltpu.bitcast(lo, U32) >> 16) | (pltpu.bitcast(hi, U32) & jnp.uint32(0xFFFF0000))
        ref[pl.ds(c, rows, stride=s_rows), :] = word


def _load_packed_rows(ref, c, rows, s_rows, row0=0):
    word = ref[pl.ds(row0 * s_rows + c, rows, stride=s_rows), :]
    return pltpu.bitcast(word << 16, F32), pltpu.bitcast(word & jnp.uint32(0xFFFF0000), F32)


def _mem_kv_kernel(mem_ref, g_ref, wk_ref, wv_ref, k_ref, v_ref):
    mn = _rms(mem_ref[0], g_ref[...]).astype(BF16)
    k_ref[0] = jnp.dot(mn, wk_ref[...], preferred_element_type=F32).astype(BF16)
    v_ref[0] = jnp.dot(mn, wv_ref[...], preferred_element_type=F32).astype(BF16)


def _mem_kv(mem, g, wk, wv):
    b, m, d = mem.shape
    xa = wk.shape[1]
    return pl.pallas_call(
        _mem_kv_kernel,
        grid=(b,),
        in_specs=[
            pl.BlockSpec((1, m, d), lambda i: (i, 0, 0)),
            _const_spec((1, d)),
            _const_spec((d, xa)),
            _const_spec((d, xa)),
        ],
        out_specs=[
            pl.BlockSpec((1, m, xa), lambda i: (i, 0, 0)),
            pl.BlockSpec((1, m, xa), lambda i: (i, 0, 0)),
        ],
        out_shape=[jax.ShapeDtypeStruct((b, m, xa), BF16), jax.ShapeDtypeStruct((b, m, xa), BF16)],
        compiler_params=pltpu.CompilerParams(
            dimension_semantics=("arbitrary",), vmem_limit_bytes=VMEM_LIMIT_BYTES),
        name="mem_kv",
    )(mem, g, wk, wv)


def _route_previous_tile(step, hnb_s, wrt_ref, br_ref, run_s, idx_ref, gate_ref, rank_ref, cnt_ref):
    n_exp = wrt_ref.shape[0]
    tt = hnb_s.shape[0]
    lg = lax.dot_general(wrt_ref[...], hnb_s[...], (((1,), (1,)), ((), ())),
                         preferred_element_type=F32) + br_ref[...]
    eid = lax.broadcasted_iota(I32, (n_exp, tt), 0)
    vals, sels = [], []
    for k in range(TOP_K):
        m = jnp.max(lg, axis=0, keepdims=True)
        ik = jnp.min(jnp.where(lg == m, eid, n_exp), axis=0, keepdims=True)
        sel = eid == ik
        vals.append(m)
        sels.append(sel)
        idx_ref[k:k + 1, :] = ik
        lg = jnp.where(sel, -jnp.inf, lg)
    es = [jnp.exp(v - vals[0]) for v in vals]
    den = es[0] + es[1] + es[2] + es[3]
    for k in range(TOP_K):
        gate_ref[k:k + 1, :] = es[k] / den
    oh = jnp.where(sels[0] | sels[1] | sels[2] | sels[3], 1.0, 0.0)
    upper = jnp.where(lax.broadcasted_iota(I32, (tt, tt), 0) < lax.broadcasted_iota(I32, (tt, tt), 1),
                      1.0, 0.0).astype(BF16)
    before = jnp.dot(oh.astype(BF16), upper, preferred_element_type=F32) + run_s[:, 0:1]
    for k in range(TOP_K):
        rank_ref[k:k + 1, :] = jnp.sum(jnp.where(sels[k], before, 0.0), axis=0, keepdims=True).astype(I32)
    counted = jnp.where(step >= 1, 1.0, 0.0)
    run_s[...] = run_s[...] + counted * jnp.sum(oh, axis=1, keepdims=True)
    cnt_ref[...] = run_s[...]


def _mixer_kernel(x_ref, lnmix_ref, win_ref, cw_ref, cb_ref, wai_ref, ba_ref, bi_ref, lam_ref,
                  gvn_ref, wsp_ref, bst_ref, k_ref, v_ref, outg_ref, wout_ref, lnffn_ref, wrt_ref, br_ref,
                  h_ref, hnp_ref, idx_ref, gate_ref, rank_ref, cnt_ref,
                  p_s, xr_s, a_s, u_s, hs_s, hc_s, y_s, yb_s, run_s, hnb_s,
                  *, rg, gm, xa, chunk, n_t):
    tt, d = x_ref.shape
    c1, c2, c3, c4 = rg, 2 * rg, 2 * rg + gm, 2 * rg + 2 * gm
    step = pl.program_id(0)

    @pl.when(step == 0)
    def _():
        run_s[...] = jnp.zeros(run_s.shape, F32)
        hnb_s[...] = jnp.zeros(hnb_s.shape, BF16)

    @pl.when(lax.rem(step, n_t) == 0)
    def _():
        xr_s[0:SUBLANES, :] = jnp.zeros((SUBLANES, rg), F32)
        hc_s[...] = jnp.zeros((SUBLANES, rg), F32)

    _route_previous_tile(step, hnb_s, wrt_ref, br_ref, run_s, idx_ref, gate_ref, rank_ref, cnt_ref)

    x = x_ref[...]
    xn = _rms(x, lnmix_ref[...]).astype(BF16)
    p_s[...] = jnp.dot(xn, win_ref[...], preferred_element_type=F32)

    xr_s[SUBLANES:SUBLANES + tt, :] = p_s[:, 0:c1]
    cw = cw_ref[...]
    kw = cw.shape[0]
    xc = cb_ref[...] + cw[0:1, :] * xr_s[pl.ds(SUBLANES - kw + 1, tt), :]
    for w in range(1, kw):
        xc = xc + cw[w:w + 1, :] * xr_s[pl.ds(SUBLANES - kw + 1 + w, tt), :]
    xr_s[0:SUBLANES, :] = xr_s[tt:tt + SUBLANES, :]

    z = -lam_ref[...]
    sp = jnp.maximum(z, 0.0) + jnp.log1p(jnp.exp(-jnp.abs(z)))
    gates = jnp.dot(xc.astype(BF16), wai_ref[...], preferred_element_type=F32)
    for hd in range(rg // HEAD_DIM):
        sl = slice(hd * HEAD_DIM, (hd + 1) * HEAD_DIM)
        r = _sigmoid(gates[:, hd * HEAD_DIM:(hd + 1) * HEAD_DIM] + ba_ref[:, sl])
        ig = _sigmoid(gates[:, rg + hd * HEAD_DIM:rg + (hd + 1) * HEAD_DIM] + bi_ref[:, sl])
        log_a = (-RG_C) * r * sp[:, sl]
        a = jnp.exp(log_a)
        a_s[:, sl] = a
        u_s[:, sl] = jnp.sqrt(-jnp.tanh(log_a) * (a * a + 1.0)) * (ig * xc[:, sl])

    row = lax.broadcasted_iota(I32, (SUBLANES, rg), 0)
    hprev = hc_s[...]
    for g in range(tt // SUBLANES):
        rows = slice(g * SUBLANES, (g + 1) * SUBLANES)
        aa = a_s[rows, :]
        bb = u_s[rows, :]
        for s in (1, 2, 4):
            keep = row >= s
            bb = jnp.where(keep, aa * pltpu.roll(bb, s, 0) + bb, bb)
            aa = jnp.where(keep, aa * pltpu.roll(aa, s, 0), aa)
        h8 = aa * hprev + bb
        hs_s[rows, :] = h8
        hprev = jnp.broadcast_to(h8[SUBLANES - 1:SUBLANES, :], (SUBLANES, rg))
    hc_s[...] = hprev
    y_s[:, 0:rg] = _gelu(p_s[:, c1:c2]) * hs_s[...]

    vn = _rms(_gelu(p_s[:, c3:c4]), gvn_ref[...]).astype(BF16)
    tri = (lax.broadcasted_iota(I32, (chunk, chunk), 0) >= lax.broadcasted_iota(I32, (chunk, chunk), 1))
    for g in range(gm // HEAD_DIM):
        wg = jnp.where(tri, wsp_ref[g], 0.0).astype(BF16)
        bcol = bst_ref[:, g:g + 1]
        for c in range(tt // chunk):
            rs = slice(c * chunk, (c + 1) * chunk)
            cs = slice(g * HEAD_DIM, (g + 1) * HEAD_DIM)
            sv = jnp.dot(wg, vn[rs, cs], preferred_element_type=F32) + bcol
            y_s[rs, rg + g * HEAD_DIM:rg + (g + 1) * HEAD_DIM] = (
                _gelu(p_s[rs, c2 + g * HEAD_DIM:c2 + (g + 1) * HEAD_DIM]) * sv)

    scale = HEAD_DIM ** -0.5
    for hd in range(xa // HEAD_DIM):
        sl = slice(hd * HEAD_DIM, (hd + 1) * HEAD_DIM)
        q = p_s[:, c4 + hd * HEAD_DIM:c4 + (hd + 1) * HEAD_DIM].astype(BF16)
        s = lax.dot_general(q, k_ref[0, :, sl], (((1,), (1,)), ((), ())),
                            preferred_element_type=F32) * scale
        e = jnp.exp(s - jnp.max(s, axis=-1, keepdims=True))
        o = jnp.dot(e.astype(BF16), v_ref[0, :, sl], preferred_element_type=F32)
        y_s[:, rg + gm + hd * HEAD_DIM:rg + gm + (hd + 1) * HEAD_DIM] = o / jnp.sum(e, axis=-1, keepdims=True)

    for lo, hi in ((0, rg), (rg, rg + gm), (rg + gm, rg + gm + xa)):
        yb_s[:, lo:hi] = _rms(y_s[:, lo:hi], outg_ref[:, lo:hi]).astype(BF16)
    h = x + jnp.dot(yb_s[...], wout_ref[...], preferred_element_type=F32)
    h_ref[...] = h

    hn = _rms(h, lnffn_ref[...])
    _store_packed_rows(hnp_ref, hn, d // (2 * LANES))
    hnb_s[...] = hn.astype(BF16)


def _mixer(x, k, v, prm, *, tt):
    b, t, d = x.shape
    n_t = t // tt
    n = b * t
    rg = prm["conv_w"].shape[1]
    gm = prm["gvn"].shape[1]
    m = k.shape[1]
    xa = k.shape[2]
    chunk = prm["wsp"].shape[1]
    in_cols = prm["w_in"].shape[1]
    n_exp = prm["wrt"].shape[0]
    s_rows = d // (2 * LANES)
    const_names = ("lnmix", "w_in", "conv_w", "conv_b", "wai", "ba", "bi", "lam", "gvn", "wsp", "bst")
    const_names2 = ("outg", "w_out", "lnffn", "wrt", "br")
    total = b * n_t
    tile_of = lambda s: jnp.minimum(s, total - 1)
    in_specs = ([pl.BlockSpec((tt, d), lambda s: (tile_of(s), 0))]
                + [_const_spec(prm[c].shape) for c in const_names]
                + [pl.BlockSpec((1, m, xa), lambda s: (tile_of(s) // n_t, 0, 0)),
                   pl.BlockSpec((1, m, xa), lambda s: (tile_of(s) // n_t, 0, 0))]
                + [_const_spec(prm[c].shape) for c in const_names2])
    routed = lambda s: (0, jnp.maximum(s - 1, 0))
    out_specs = [
        pl.BlockSpec((tt, d), lambda s: (s, 0)),
        pl.BlockSpec((tt * s_rows, LANES), lambda s: (s, 0)),
        pl.BlockSpec((TOP_K, tt), routed),
        pl.BlockSpec((TOP_K, tt), routed),
        pl.BlockSpec((TOP_K, tt), routed),
        pl.BlockSpec((n_exp, LANES), lambda s: (0, 0)),
    ]
    out_shape = [
        jax.ShapeDtypeStruct((n + tt, d), F32),
        jax.ShapeDtypeStruct(((n + tt) * s_rows, LANES), U32),
        jax.ShapeDtypeStruct((TOP_K, n), I32),
        jax.ShapeDtypeStruct((TOP_K, n), F32),
        jax.ShapeDtypeStruct((TOP_K, n), I32),
        jax.ShapeDtypeStruct((n_exp, LANES), F32),
    ]
    scratch = [
        pltpu.VMEM((tt, in_cols), F32),
        pltpu.VMEM((tt + 2 * SUBLANES, rg), F32),
        pltpu.VMEM((tt, rg), F32),
        pltpu.VMEM((tt, rg), F32),
        pltpu.VMEM((tt, rg), F32),
        pltpu.VMEM((SUBLANES, rg), F32),
        pltpu.VMEM((tt, rg + gm + xa), F32),
        pltpu.VMEM((tt, rg + gm + xa), BF16),
        pltpu.VMEM((n_exp, LANES), F32),
        pltpu.VMEM((tt, d), BF16),
    ]
    return pl.pallas_call(
        functools.partial(_mixer_kernel, rg=rg, gm=gm, xa=xa, chunk=chunk, n_t=n_t),
        grid=(total + 1,),
        in_specs=in_specs,
        out_specs=out_specs,
        out_shape=out_shape,
        scratch_shapes=scratch,
        compiler_params=pltpu.CompilerParams(
            dimension_semantics=("arbitrary",), vmem_limit_bytes=VMEM_LIMIT_BYTES),
        name="mixer",
    )(x.reshape(n, d), *[prm[c] for c in const_names], k, v, *[prm[c] for c in const_names2])


def _dispatch_kernel(zs_ref, zl_ref, na_ref, dest_ref, hnp_ref, xs_ref, zb, sem, zsem, *, tile, lag, tm, s_rows):
    step = pl.program_id(0)
    n_exp = zs_ref.shape[0]
    nb = xs_ref.shape[0] // (tm * s_rows)

    def for_each_zero_copy(act):
        def per_expert(e, carry):
            off = zs_ref[e]
            length = zl_ref[e]
            p = tm // 2
            while p >= 1:
                bit = length & p

                @pl.when(bit != 0)
                def _(off=off, p=p):
                    dst = xs_ref.at[pl.ds(pl.multiple_of(off * s_rows, s_rows), p * s_rows)]
                    act(pltpu.make_async_copy(zb.at[pl.ds(0, p * s_rows)], dst, zsem))
                off = off + bit
                p //= 2
            return carry
        lax.fori_loop(0, n_exp, per_expert, 0)

        def per_block(b, carry):
            dst = xs_ref.at[pl.ds(pl.multiple_of(b * (tm * s_rows), tm * s_rows), tm * s_rows)]
            act(pltpu.make_async_copy(zb, dst, zsem))
            return carry
        lax.fori_loop(na_ref[0], nb, per_block, 0)

    @pl.when(step == 0)
    def _():
        zb[...] = jnp.zeros(zb.shape, U32)
        for_each_zero_copy(lambda c: c.start())

    def row_copy(n, dst_row):
        src = hnp_ref.at[pl.ds(pl.multiple_of(n * s_rows, s_rows), s_rows)]
        return pltpu.make_async_copy(src, xs_ref.at[pl.ds(pl.multiple_of(dst_row, s_rows), s_rows)], sem)

    def wait_token():
        for _ in range(TOP_K):
            row_copy(0, 0).wait()

    def body(n, carry):
        for k in range(TOP_K):
            row_copy(n, dest_ref[k * tile + n]).start(priority=k % 2)

        @pl.when(n >= lag)
        def _():
            wait_token()
        return carry

    lax.fori_loop(0, tile, body, 0)

    def drain(_, carry):
        wait_token()
        return carry

    lax.fori_loop(0, lag, drain, 0)

    @pl.when(step == 0)
    def _():
        for_each_zero_copy(lambda c: c.wait())


def _dispatch(zero_start, zero_len, n_active, dest_rows_tiles, hnp, rows, *, tile, tm, s_rows):
    n = dest_rows_tiles.shape[0] // TOP_K
    lag = min(DISPATCH_LAG, tile)
    grid_spec = pltpu.PrefetchScalarGridSpec(
        num_scalar_prefetch=3,
        grid=(n // tile,),
        in_specs=[
            pl.BlockSpec((TOP_K * tile,), lambda i, *_: (i,), memory_space=pltpu.SMEM),
            pl.BlockSpec((tile * s_rows, LANES), lambda i, *_: (i, 0)),
        ],
        out_specs=pl.BlockSpec(memory_space=pl.ANY),
        scratch_shapes=[pltpu.VMEM((tm * s_rows, LANES), U32), pltpu.SemaphoreType.DMA(()),
                        pltpu.SemaphoreType.DMA(())],
    )
    return pl.pallas_call(
        functools.partial(_dispatch_kernel, tile=tile, lag=lag, tm=tm, s_rows=s_rows),
        grid_spec=grid_spec,
        out_shape=jax.ShapeDtypeStruct((rows * s_rows, LANES), U32),
        compiler_params=pltpu.CompilerParams(
            dimension_semantics=("arbitrary",), vmem_limit_bytes=VMEM_LIMIT_BYTES,
            disable_bounds_checks=True, has_side_effects=True),
        name="dispatch",
    )(zero_start, zero_len, n_active, dest_rows_tiles, hnp)


def _expert_changed(be_ref, i):
    prev = be_ref[jnp.maximum(i - 1, 0)]
    return (i == 0) | (be_ref[i] != prev)


def _weight_ring_step(w_hbm, wbuf, wsem, be_ref, nxt_ref, rid_ref, nr_ref, i, j, n_j):
    col_tile = wbuf.shape[2]
    run = j * nr_ref[0] + rid_ref[i]
    slot = run & 1

    def tile_copy(e, jj, sl):
        cols = pl.ds(pl.multiple_of(jj * col_tile, col_tile), col_tile)
        return pltpu.make_async_copy(w_hbm.at[e, :, cols], wbuf.at[sl], wsem.at[sl])

    @pl.when(run == 0)
    def _():
        tile_copy(be_ref[0], 0, 0).start()

    tile_copy(be_ref[i], j, slot).wait()
    nxt = nxt_ref[i]

    @pl.when(nxt >= 0)
    def _():
        tile_copy(nxt, j, 1 - slot).start()

    @pl.when((nxt < 0) & (j + 1 < n_j))
    def _():
        tile_copy(be_ref[0], j + 1, 1 - slot).start()

    return slot


def _for_filled_rows(active, nq, tm, body):
    quantum = min(MOE_QUANTUM, tm)
    for q in range(1, tm // quantum + 1):
        @pl.when(active & (nq == q))
        def _(q=q):
            body(q * quantum)


def _moe_up_kernel(be_ref, na_ref, nxt_ref, rid_ref, nr_ref, nq_ref, xs_ref, w_hbm, bg_ref, bu_ref, perm_ref,
                   act_ref, wbuf, wsem, wp_s, xb_s):
    j = pl.program_id(0)
    i = pl.program_id(1)
    tn = wbuf.shape[2]
    hn_ = tn // 2
    d = xb_s.shape[1]
    active = i < na_ref[0]

    @pl.when(active & _expert_changed(be_ref, i))
    def _():
        slot = _weight_ring_step(w_hbm, wbuf, wsem, be_ref, nxt_ref, rid_ref, nr_ref, i, j, pl.num_programs(0))
        for g in range(tn // MXU_DIM):
            wg = wbuf[slot, :, g * MXU_DIM:(g + 1) * MXU_DIM].astype(BF16)
            wq = jnp.dot(wg, perm_ref[...], preferred_element_type=F32).astype(BF16)
            hw = MXU_DIM // 2
            wp_s[:, g * hw:(g + 1) * hw] = wq[:, 0:hw]
            wp_s[:, hn_ + g * hw:hn_ + (g + 1) * hw] = wq[:, hw:MXU_DIM]

    tm = xb_s.shape[0]

    def filled(rows):
        half = d // 2
        s_rows = half // LANES
        for c in range(s_rows):
            lo, hi = _load_packed_rows(xs_ref, c, rows, s_rows)
            xb_s[0:rows, c * LANES:(c + 1) * LANES] = lo.astype(BF16)
            xb_s[0:rows, half + c * LANES:half + (c + 1) * LANES] = hi.astype(BF16)
        gu = jnp.dot(xb_s[0:rows, :], wp_s[...], preferred_element_type=F32)
        gate = jnp.minimum(gu[:, 0:hn_] + bg_ref[0], SWIGLU_LIMIT)
        up = jnp.clip(gu[:, hn_:tn] + bu_ref[0], -SWIGLU_LIMIT, SWIGLU_LIMIT)
        act_ref[0:rows, :] = ((up + 1.0) * (gate * _sigmoid(SWIGLU_ALPHA * gate))).astype(BF16)
        if rows < tm:
            act_ref[rows:tm, :] = jnp.zeros((tm - rows, act_ref.shape[1]), BF16)

    _for_filled_rows(active, nq_ref[i], tm, filled)

    @pl.when(jnp.logical_not(active))
    def _():
        act_ref[...] = jnp.zeros(act_ref.shape, BF16)


def _moe_up(plan, xs, w_gate_up, bg, bu, perm, *, tm, tn):
    n_exp, d, f2 = w_gate_up.shape
    s_rows = d // (2 * LANES)
    rows = xs.shape[0] // s_rows
    nb = rows // tm

    def blk(i, na):
        return jnp.minimum(i, na[0] - 1)

    grid_spec = pltpu.PrefetchScalarGridSpec(
        num_scalar_prefetch=len(plan),
        grid=(f2 // tn, nb),
        in_specs=[
            pl.BlockSpec((tm * s_rows, LANES), lambda j, i, be, na, *_: (blk(i, na), 0)),
            pl.BlockSpec(memory_space=pl.ANY),
            pl.BlockSpec((1, 1, tn // 2), lambda j, i, be, na, *_: (be[blk(i, na)], 0, j)),
            pl.BlockSpec((1, 1, tn // 2), lambda j, i, be, na, *_: (be[blk(i, na)], 0, j)),
            pl.BlockSpec((MXU_DIM, MXU_DIM), lambda j, i, *_: (0, 0)),
        ],
        out_specs=pl.BlockSpec((tm, tn // 2), lambda j, i, *_: (i, j)),
        scratch_shapes=[pltpu.VMEM((2, d, tn), F32), pltpu.SemaphoreType.DMA((2,)),
                        pltpu.VMEM((d, tn), BF16), pltpu.VMEM((tm, d), BF16)],
    )
    return pl.pallas_call(
        _moe_up_kernel,
        grid_spec=grid_spec,
        out_shape=jax.ShapeDtypeStruct((rows, f2 // 2), BF16),
        compiler_params=pltpu.CompilerParams(
            dimension_semantics=("arbitrary", "arbitrary"), vmem_limit_bytes=VMEM_LIMIT_BYTES),
        name="moe_up",
    )(*plan, xs, w_gate_up, bg, bu, perm)


def _moe_down_kernel(be_ref, na_ref, nxt_ref, rid_ref, nr_ref, nq_ref, act_ref, w_hbm, b_ref, ys_ref,
                     wbuf, wsem, wb_s):
    i = pl.program_id(0)
    active = i < na_ref[0]
    tm = act_ref.shape[0]
    s_rows = wb_s.shape[1] // (2 * LANES)

    @pl.when(active & _expert_changed(be_ref, i))
    def _():
        slot = _weight_ring_step(w_hbm, wbuf, wsem, be_ref, nxt_ref, rid_ref, nr_ref, i, 0, 1)
        wb_s[...] = wbuf[slot].astype(BF16)

    def filled(rows):
        y = jnp.dot(act_ref[0:rows, :], wb_s[...], preferred_element_type=F32) + b_ref[0]
        _store_packed_rows(ys_ref, y, s_rows)
        if rows < tm:
            ys_ref[rows * s_rows:tm * s_rows, :] = jnp.zeros(((tm - rows) * s_rows, LANES), U32)

    _for_filled_rows(active, nq_ref[i], tm, filled)

    @pl.when(jnp.logical_not(active))
    def _():
        ys_ref[...] = jnp.zeros(ys_ref.shape, U32)


def _moe_down(plan, act, w_down, b_down, *, tm):
    rows, f = act.shape
    n_exp, _, d = w_down.shape
    s_rows = d // (2 * LANES)

    def blk(i, na):
        return jnp.minimum(i, na[0] - 1)

    grid_spec = pltpu.PrefetchScalarGridSpec(
        num_scalar_prefetch=len(plan),
        grid=(rows // tm,),
        in_specs=[
            pl.BlockSpec((tm, f), lambda i, be, na, *_: (blk(i, na), 0)),
            pl.BlockSpec(memory_space=pl.ANY),
            pl.BlockSpec((1, 1, d), lambda i, be, na, *_: (be[blk(i, na)], 0, 0)),
        ],
        out_specs=pl.BlockSpec((tm * s_rows, LANES), lambda i, *_: (i, 0)),
        scratch_shapes=[pltpu.VMEM((2, f, d), F32), pltpu.SemaphoreType.DMA((2,)), pltpu.VMEM((f, d), BF16)],
    )
    return pl.pallas_call(
        _moe_down_kernel,
        grid_spec=grid_spec,
        out_shape=jax.ShapeDtypeStruct((rows * s_rows, LANES), U32),
        compiler_params=pltpu.CompilerParams(
            dimension_semantics=("arbitrary",), vmem_limit_bytes=VMEM_LIMIT_BYTES),
        name="moe_down",
    )(*plan, act, w_down, b_down)


def _combine_kernel(dcur_ref, dnext_ref, ys_ref, h_ref, gates_ref, fg_ref, out_ref, buf, sem, *, tile, s_rows):
    s = pl.program_id(0)
    n_steps = pl.num_programs(0)

    slot_rows = TOP_K * tile * s_rows

    def row_copy(slot, k, n, src_row):
        src = ys_ref.at[pl.ds(pl.multiple_of(src_row, s_rows), s_rows)]
        dst = buf.at[slot, pl.ds(pl.multiple_of((k * tile + n) * s_rows, s_rows), s_rows)]
        return pltpu.make_async_copy(src, dst, sem.at[slot])

    def issue(d_ref, slot):
        def body(n, carry):
            for k in range(TOP_K):
                row_copy(slot, k, n, d_ref[k * tile + n]).start(priority=k % 2)
            return carry
        lax.fori_loop(0, tile, body, 0, unroll=ISSUE_UNROLL)

    @pl.when(s == 0)
    def _():
        issue(dcur_ref, 0)

    @pl.when(s + 1 < n_steps)
    def _():
        issue(dnext_ref, (s + 1) % 2)

    def consume(slot):
        pltpu.make_async_copy(ys_ref.at[pl.ds(0, slot_rows)], buf.at[slot], sem.at[slot]).wait()
        gs = [gates_ref[:, k:k + 1] for k in range(TOP_K)]
        d = out_ref.shape[1]
        half = d // 2
        ssq = jnp.zeros((tile, 1), F32)
        for c in range(s_rows):
            lo_cols = slice(c * LANES, (c + 1) * LANES)
            hi_cols = slice(half + c * LANES, half + (c + 1) * LANES)
            acc_lo = h_ref[:, lo_cols]
            acc_hi = h_ref[:, hi_cols]
            for k in range(TOP_K):
                lo, hi = _load_packed_rows(buf.at[slot], c, tile, s_rows, row0=k * tile)
                acc_lo = acc_lo + gs[k] * lo
                acc_hi = acc_hi + gs[k] * hi
            out_ref[:, lo_cols] = acc_lo
            out_ref[:, hi_cols] = acc_hi
            ssq = ssq + jnp.sum(acc_lo * acc_lo + acc_hi * acc_hi, axis=-1, keepdims=True)
        out_ref[...] = out_ref[...] * lax.rsqrt(ssq / d + EPS) * fg_ref[...]

    for slot in range(2):
        @pl.when(s % 2 == slot)
        def _(slot=slot):
            consume(slot)


def _combine(src_rows_tiles, ys, h2, gates_t, fg, *, n, tile):
    d = h2.shape[1]
    s_rows = d // (2 * LANES)
    n_steps = n // tile
    return pl.pallas_call(
        functools.partial(_combine_kernel, tile=tile, s_rows=s_rows),
        grid=(n_steps,),
        in_specs=[
            pl.BlockSpec((TOP_K * tile,), lambda i: (i,), memory_space=pltpu.SMEM),
            pl.BlockSpec((TOP_K * tile,), lambda i: (jnp.minimum(i + 1, n_steps - 1),), memory_space=pltpu.SMEM),
            pl.BlockSpec(memory_space=pl.ANY),
            pl.BlockSpec((tile, d), lambda i: (i, 0)),
            pl.BlockSpec((tile, TOP_K), lambda i: (i, 0)),
            _const_spec((1, d)),
        ],
        out_specs=pl.BlockSpec((tile, d), lambda i: (i, 0)),
        out_shape=jax.ShapeDtypeStruct((n, d), F32),
        scratch_shapes=[pltpu.VMEM((2, TOP_K * tile * s_rows, LANES), U32), pltpu.SemaphoreType.DMA((2,))],
        compiler_params=pltpu.CompilerParams(
            dimension_semantics=("arbitrary",), vmem_limit_bytes=VMEM_LIMIT_BYTES,
            disable_bounds_checks=True),
        name="combine",
    )(src_rows_tiles, src_rows_tiles, ys, h2, gates_t, fg)


def _tile_major(a, tile):
    k, n = a.shape
    return a.reshape(k, n // tile, tile).transpose(1, 0, 2).reshape(-1)


def _layer(h, mem, ln_mix_g, w_in, conv_w, conv_b, w_rg_a, b_rg_a, w_rg_i, b_rg_i, lru_lambda, gm_v_norm_g,
           w_spatial, b_spatial, mem_norm_g, w_mem_k, w_mem_v, out_norm_g, w_out, ln_ffn_g, w_router, b_router,
           w_gate_up, b_gate_up, w_down, b_down, final_g, *, mix_tile, moe_tile, up_tn,
           dispatch_tile, combine_tile):
    b, t, d = h.shape
    n = b * t
    n_exp = w_router.shape[1]
    row = lambda a: a.reshape(1, -1)
    km, v = _mem_kv(mem, row(mem_norm_g), w_mem_k.astype(BF16), w_mem_v.astype(BF16))
    heads = w_rg_a.shape[0]
    eye = jnp.eye(heads, dtype=w_rg_a.dtype)

    def block_diag(w):
        return (eye[:, None, :, None] * w[:, :, None, :]).reshape(heads * w.shape[1], heads * w.shape[2])

    prm = dict(
        lnmix=row(ln_mix_g), w_in=w_in.astype(BF16), conv_w=conv_w, conv_b=row(conv_b),
        wai=jnp.concatenate([block_diag(w_rg_a), block_diag(w_rg_i)], axis=1).astype(BF16),
        ba=row(b_rg_a), bi=row(b_rg_i), lam=row(lru_lambda),
        gvn=row(gm_v_norm_g), wsp=w_spatial, bst=b_spatial.T, outg=row(out_norm_g), w_out=w_out.astype(BF16),
        lnffn=row(ln_ffn_g), wrt=w_router.T.astype(BF16), br=b_router.reshape(-1, 1))
    h1, hnp, idx, gates, rank, cnt = _mixer(h, km, v, prm, tt=mix_tile)

    counts = cnt[:, 0].astype(I32)
    padded = ((counts + moe_tile - 1) // moe_tile) * moe_tile
    pad_end = jnp.cumsum(padded)
    pad_start = pad_end - padded
    s_rows = d // (2 * LANES)
    experts = jnp.arange(n_exp, dtype=I32)
    start_of = jnp.sum(jnp.where(idx[..., None] == experts, pad_start, 0), axis=-1)
    dest_rows = (start_of + rank) * s_rows
    nb = (n * TOP_K) // moe_tile + n_exp
    n_active = (pad_end[-1] // moe_tile).astype(I32).reshape(1)
    block_first = jnp.arange(nb, dtype=I32) * moe_tile
    block_expert = jnp.minimum(
        jnp.sum((pad_end[None, :] <= block_first[:, None]).astype(I32), axis=-1), n_exp - 1)
    blocks = jnp.arange(nb, dtype=I32)
    starts_run = (blocks < n_active[0]) & ((blocks == 0) | (block_expert != jnp.roll(block_expert, 1)))
    run_id = jnp.cumsum(starts_run.astype(I32)) - 1
    later_start = lax.cummin(jnp.where(starts_run, blocks, nb)[::-1])[::-1]
    next_start = jnp.concatenate([later_start[1:], jnp.full((1,), nb, I32)])
    next_expert = jnp.where(next_start < nb, block_expert[jnp.minimum(next_start, nb - 1)], -1).astype(I32)
    n_runs = jnp.sum(starts_run.astype(I32)).reshape(1)
    filled_rows = jnp.clip(pad_start[block_expert] + counts[block_expert] - block_first, 1, moe_tile)
    quantum = min(MOE_QUANTUM, moe_tile)
    n_quanta = ((filled_rows + quantum - 1) // quantum).astype(I32)
    plan = (block_expert, n_active, next_expert, run_id.astype(I32), n_runs, n_quanta)

    xs = _dispatch(pad_start + counts, padded - counts, n_active, _tile_major(dest_rows, dispatch_tile), hnp,
                   nb * moe_tile, tile=dispatch_tile, tm=moe_tile, s_rows=s_rows)
    f = w_down.shape[1]
    r_ = jnp.arange(MXU_DIM)
    src = jnp.where(r_ < MXU_DIM // 2, 2 * r_, 2 * (r_ - MXU_DIM // 2) + 1)
    perm = (jnp.arange(MXU_DIM)[:, None] == src[None, :]).astype(BF16)
    bg = b_gate_up[:, 0::2].reshape(n_exp, 1, f)
    bu = b_gate_up[:, 1::2].reshape(n_exp, 1, f)
    act = _moe_up(plan, xs, w_gate_up, bg, bu, perm, tm=moe_tile, tn=up_tn)
    ys = _moe_down(plan, act, w_down, b_down.reshape(n_exp, 1, d), tm=moe_tile)
    out = _combine(_tile_major(dest_rows, combine_tile), ys, h1, gates.T, row(final_g), n=n, tile=combine_tile)
    return out.reshape(b, t, d)


def kernel(x, mem, ln_mix_g, w_in, conv_w, conv_b, w_rg_a, b_rg_a, w_rg_i, b_rg_i, lru_lambda, gm_v_norm_g, w_spatial, b_spatial, mem_norm_g, w_mem_k, w_mem_v, out_norm_g, w_out, ln_ffn_g, w_router, b_router, w_gate_up, b_gate_up, w_down, b_down, final_norm_g):
    depth = w_in.shape[0]
    assert depth == 1, "the final RMSNorm is fused into the single layer's combine stage"
    return _layer(
        x, mem, ln_mix_g[0], w_in[0], conv_w[0], conv_b[0], w_rg_a[0], b_rg_a[0], w_rg_i[0], b_rg_i[0],
        lru_lambda[0], gm_v_norm_g[0], w_spatial[0], b_spatial[0], mem_norm_g[0], w_mem_k[0], w_mem_v[0],
        out_norm_g[0], w_out[0], ln_ffn_g[0], w_router[0], b_router[0], w_gate_up[0], b_gate_up[0], w_down[0],
        b_down[0], final_norm_g,
        mix_tile=MIX_TILE, moe_tile=MOE_TILE, up_tn=UP_TN,
        dispatch_tile=DISPATCH_TILE, combine_tile=COMBINE_TILE)
```

```python
import functools

import jax
import jax.numpy as jnp
from jax import lax
from jax.experimental import pallas as pl
from jax.experimental.pallas import tpu as pltpu

F32 = jnp.float32
BF16 = jnp.bfloat16
I32 = jnp.int32
U32 = jnp.uint32

HEAD_DIM = 128
RG_C = 8.0
TOP_K = 4
SWIGLU_LIMIT = 7.0
SWIGLU_ALPHA = 1.702
EPS = 1e-6

LANES = 128
SUBLANES = 8
MXU_DIM = 256
VMEM_LIMIT_BYTES = 56 * 1024 * 1024

MIX_TILE = 256
MOE_TILE = 512
MOE_QUANTUM = 128
UP_TN = 2048
DISPATCH_TILE = 512
DISPATCH_LAG = 64
COMBINE_TILE = 256
ISSUE_UNROLL = 4


def _rms(x, g):
    return x * lax.rsqrt(jnp.mean(x * x, axis=-1, keepdims=True) + EPS) * g


def _gelu(x):
    return x * (0.5 * (1.0 + jnp.tanh(0.7978845608028654 * (x + 0.044715 * (x * x * x)))))


def _sigmoid(x):
    return 1.0 / (1.0 + jnp.exp(-x))


def _const_spec(shape):
    nd = len(shape)
    return pl.BlockSpec(shape, lambda *_: (0,) * nd, pipeline_mode=pl.Buffered(1))


def _store_packed_rows(ref, y, s_rows):
    rows, d = y.shape
    half = d // 2
    for c in range(s_rows):
        lo = y[:, c * LANES:(c + 1) * LANES].astype(BF16).astype(F32)
        hi = y[:, half + c * LANES:half + (c + 1) * LANES].astype(BF16).astype(F32)
        word = (pltpu.bitcast(lo, U32) >> 16) | (pltpu.bitcast(hi, U32) & jnp.uint32(0xFFFF0000))
        ref[pl.ds(c, rows, stride=s_rows), :] = word


def _load_packed_rows(ref, c, rows, s_rows, row0=0):
    word = ref[pl.ds(row0 * s_rows + c, rows, stride=s_rows), :]
    return pltpu.bitcast(word << 16, F32), pltpu.bitcast(word & jnp.uint32(0xFFFF0000), F32)


def _mem_kv_kernel(mem_ref, g_ref, wk_ref, wv_ref, k_ref, v_ref):
    mn = _rms(mem_ref[0], g_ref[...]).astype(BF16)
    k_ref[0] = jnp.dot(mn, wk_ref[...], preferred_element_type=F32).astype(BF16)
    v_ref[0] = jnp.dot(mn, wv_ref[...], preferred_element_type=F32).astype(BF16)


def _mem_kv(mem, g, wk, wv):
    b, m, d = mem.shape
    xa = wk.shape[1]
    return pl.pallas_call(
        _mem_kv_kernel,
        grid=(b,),
        in_specs=[
            pl.BlockSpec((1, m, d), lambda i: (i, 0, 0)),
            _const_spec((1, d)),
            _const_spec((d, xa)),
            _const_spec((d, xa)),
        ],
        out_specs=[
            pl.BlockSpec((1, m, xa), lambda i: (i, 0, 0)),
            pl.BlockSpec((1, m, xa), lambda i: (i, 0, 0)),
        ],
        out_shape=[jax.ShapeDtypeStruct((b, m, xa), BF16), jax.ShapeDtypeStruct((b, m, xa), BF16)],
        compiler_params=pltpu.CompilerParams(
            dimension_semantics=("arbitrary",), vmem_limit_bytes=VMEM_LIMIT_BYTES),
        name="mem_kv",
    )(mem, g, wk, wv)


def _route_previous_tile(step, hnb_s, wrt_ref, br_ref, run_s, idx_ref, gate_ref, rank_ref, cnt_ref):
    n_exp = wrt_ref.shape[0]
    tt = hnb_s.shape[0]
    lg = lax.dot_general(wrt_ref[...], hnb_s[...], (((1,), (1,)), ((), ())),
                         preferred_element_type=F32) + br_ref[...]
    eid = lax.broadcasted_iota(I32, (n_exp, tt), 0)
    vals, sels = [], []
    for k in range(TOP_K):
        m = jnp.max(lg, axis=0, keepdims=True)
        ik = jnp.min(jnp.where(lg == m, eid, n_exp), axis=0, keepdims=True)
        sel = eid == ik
        vals.append(m)
        sels.append(sel)
        idx_ref[k:k + 1, :] = ik
        lg = jnp.where(sel, -jnp.inf, lg)
    es = [jnp.exp(v - vals[0]) for v in vals]
    den = es[0] + es[1] + es[2] + es[3]
    for k in range(TOP_K):
        gate_ref[k:k + 1, :] = es[k] / den
    oh = jnp.where(sels[0] | sels[1] | sels[2] | sels[3], 1.0, 0.0)
    upper = jnp.where(lax.broadcasted_iota(I32, (tt, tt), 0) < lax.broadcasted_iota(I32, (tt, tt), 1),
                      1.0, 0.0).astype(BF16)
    before = jnp.dot(oh.astype(BF16), upper, preferred_element_type=F32) + run_s[:, 0:1]
    for k in range(TOP_K):
        rank_ref[k:k + 1, :] = jnp.sum(jnp.where(sels[k], before, 0.0), axis=0, keepdims=True).astype(I32)
    counted = jnp.where(step >= 1, 1.0, 0.0)
    run_s[...] = run_s[...] + counted * jnp.sum(oh, axis=1, keepdims=True)
    cnt_ref[...] = run_s[...]


def _mixer_kernel(x_ref, lnmix_ref, win_ref, cw_ref, cb_ref, wai_ref, ba_ref, bi_ref, lam_ref,
                  gvn_ref, wsp_ref, bst_ref, k_ref, v_ref, outg_ref, wout_ref, lnffn_ref, wrt_ref, br_ref,
                  h_ref, hnp_ref, idx_ref, gate_ref, rank_ref, cnt_ref,
                  p_s, xr_s, a_s, u_s, hs_s, hc_s, y_s, yb_s, run_s, hnb_s,
                  *, rg, gm, xa, chunk, n_t):
    tt, d = x_ref.shape
    c1, c2, c3, c4 = rg, 2 * rg, 2 * rg + gm, 2 * rg + 2 * gm
    step = pl.program_id(0)

    @pl.when(step == 0)
    def _():
        run_s[...] = jnp.zeros(run_s.shape, F32)
        hnb_s[...] = jnp.zeros(hnb_s.shape, BF16)

    @pl.when(lax.rem(step, n_t) == 0)
    def _():
        xr_s[0:SUBLANES, :] = jnp.zeros((SUBLANES, rg), F32)
        hc_s[...] = jnp.zeros((SUBLANES, rg), F32)

    _route_previous_tile(step, hnb_s, wrt_ref, br_ref, run_s, idx_ref, gate_ref, rank_ref, cnt_ref)

    x = x_ref[...]
    xn = _rms(x, lnmix_ref[...]).astype(BF16)
    p_s[...] = jnp.dot(xn, win_ref[...], preferred_element_type=F32)

    xr_s[SUBLANES:SUBLANES + tt, :] = p_s[:, 0:c1]
    cw = cw_ref[...]
    kw = cw.shape[0]
    xc = cb_ref[...] + cw[0:1, :] * xr_s[pl.ds(SUBLANES - kw + 1, tt), :]
    for w in range(1, kw):
        xc = xc + cw[w:w + 1, :] * xr_s[pl.ds(SUBLANES - kw + 1 + w, tt), :]
    xr_s[0:SUBLANES, :] = xr_s[tt:tt + SUBLANES, :]

    z = -lam_ref[...]
    sp = jnp.maximum(z, 0.0) + jnp.log1p(jnp.exp(-jnp.abs(z)))
    gw = wai_ref.shape[1]
    for grp in range(rg // gw):
        sl = slice(grp * gw, (grp + 1) * gw)
        gates = jnp.dot(xc[:, sl].astype(BF16), wai_ref[grp], preferred_element_type=F32)
        r = _sigmoid(gates[:, 0:gw] + ba_ref[:, sl])
        ig = _sigmoid(gates[:, gw:2 * gw] + bi_ref[:, sl])
        log_a = (-RG_C) * r * sp[:, sl]
        a = jnp.exp(log_a)
        a_s[:, sl] = a
        u_s[:, sl] = jnp.sqrt(-jnp.tanh(log_a) * (a * a + 1.0)) * (ig * xc[:, sl])

    row = lax.broadcasted_iota(I32, (SUBLANES, rg), 0)
    hprev = hc_s[...]
    for g in range(tt // SUBLANES):
        rows = slice(g * SUBLANES, (g + 1) * SUBLANES)
        aa = a_s[rows, :]
        bb = u_s[rows, :]
        for s in (1, 2, 4):
            keep = row >= s
            bb = jnp.where(keep, aa * pltpu.roll(bb, s, 0) + bb, bb)
            aa = jnp.where(keep, aa * pltpu.roll(aa, s, 0), aa)
        h8 = aa * hprev + bb
        hs_s[rows, :] = h8
        hprev = jnp.broadcast_to(h8[SUBLANES - 1:SUBLANES, :], (SUBLANES, rg))
    hc_s[...] = hprev
    y_s[:, 0:rg] = _gelu(p_s[:, c1:c2]) * hs_s[...]

    vn = _rms(_gelu(p_s[:, c3:c4]), gvn_ref[...]).astype(BF16)
    tri = (lax.broadcasted_iota(I32, (chunk, chunk), 0) >= lax.broadcasted_iota(I32, (chunk, chunk), 1))
    for g in range(gm // HEAD_DIM):
        wg = jnp.where(tri, wsp_ref[g], 0.0).astype(BF16)
        bcol = bst_ref[:, g:g + 1]
        for c in range(tt // chunk):
            rs = slice(c * chunk, (c + 1) * chunk)
            cs = slice(g * HEAD_DIM, (g + 1) * HEAD_DIM)
            sv = jnp.dot(wg, vn[rs, cs], preferred_element_type=F32) + bcol
            y_s[rs, rg + g * HEAD_DIM:rg + (g + 1) * HEAD_DIM] = (
                _gelu(p_s[rs, c2 + g * HEAD_DIM:c2 + (g + 1) * HEAD_DIM]) * sv)

    scale = HEAD_DIM ** -0.5
    for hd in range(xa // HEAD_DIM):
        sl = slice(hd * HEAD_DIM, (hd + 1) * HEAD_DIM)
        q = p_s[:, c4 + hd * HEAD_DIM:c4 + (hd + 1) * HEAD_DIM].astype(BF16)
        s = lax.dot_general(q, k_ref[0, :, sl], (((1,), (1,)), ((), ())),
                            preferred_element_type=F32) * scale
        e = jnp.exp(s - jnp.max(s, axis=-1, keepdims=True))
        o = jnp.dot(e.astype(BF16), v_ref[0, :, sl], preferred_element_type=F32)
        y_s[:, rg + gm + hd * HEAD_DIM:rg + gm + (hd + 1) * HEAD_DIM] = o / jnp.sum(e, axis=-1, keepdims=True)

    for lo, hi in ((0, rg), (rg, rg + gm), (rg + gm, rg + gm + xa)):
        yb_s[:, lo:hi] = _rms(y_s[:, lo:hi], outg_ref[:, lo:hi]).astype(BF16)
    h = x + jnp.dot(yb_s[...], wout_ref[...], preferred_element_type=F32)
    h_ref[...] = h

    hn = _rms(h, lnffn_ref[...])
    _store_packed_rows(hnp_ref, hn, d // (2 * LANES))
    hnb_s[...] = hn.astype(BF16)


def _mixer(x, k, v, prm, *, tt):
    b, t, d = x.shape
    n_t = t // tt
    n = b * t
    rg = prm["conv_w"].shape[1]
    gm = prm["gvn"].shape[1]
    m = k.shape[1]
    xa = k.shape[2]
    chunk = prm["wsp"].shape[1]
    in_cols = prm["w_in"].shape[1]
    n_exp = prm["wrt"].shape[0]
    s_rows = d // (2 * LANES)
    const_names = ("lnmix", "w_in", "conv_w", "conv_b", "wai", "ba", "bi", "lam", "gvn", "wsp", "bst")
    const_names2 = ("outg", "w_out", "lnffn", "wrt", "br")
    total = b * n_t
    tile_of = lambda s: jnp.minimum(s, total - 1)
    in_specs = ([pl.BlockSpec((tt, d), lambda s: (tile_of(s), 0))]
                + [_const_spec(prm[c].shape) for c in const_names]
                + [pl.BlockSpec((1, m, xa), lambda s: (tile_of(s) // n_t, 0, 0)),
                   pl.BlockSpec((1, m, xa), lambda s: (tile_of(s) // n_t, 0, 0))]
                + [_const_spec(prm[c].shape) for c in const_names2])
    routed = lambda s: (0, jnp.maximum(s - 1, 0))
    out_specs = [
        pl.BlockSpec((tt, d), lambda s: (s, 0)),
        pl.BlockSpec((tt * s_rows, LANES), lambda s: (s, 0)),
        pl.BlockSpec((TOP_K, tt), routed),
        pl.BlockSpec((TOP_K, tt), routed),
        pl.BlockSpec((TOP_K, tt), routed),
        pl.BlockSpec((n_exp, LANES), lambda s: (0, 0)),
    ]
    out_shape = [
        jax.ShapeDtypeStruct((n + tt, d), F32),
        jax.ShapeDtypeStruct(((n + tt) * s_rows, LANES), U32),
        jax.ShapeDtypeStruct((TOP_K, n), I32),
        jax.ShapeDtypeStruct((TOP_K, n), F32),
        jax.ShapeDtypeStruct((TOP_K, n), I32),
        jax.ShapeDtypeStruct((n_exp, LANES), F32),
    ]
    scratch = [
        pltpu.VMEM((tt, in_cols), F32),
        pltpu.VMEM((tt + 2 * SUBLANES, rg), F32),
        pltpu.VMEM((tt, rg), F32),
        pltpu.VMEM((tt, rg), F32),
        pltpu.VMEM((tt, rg), F32),
        pltpu.VMEM((SUBLANES, rg), F32),
        pltpu.VMEM((tt, rg + gm + xa), F32),
        pltpu.VMEM((tt, rg + gm + xa), BF16),
        pltpu.VMEM((n_exp, LANES), F32),
        pltpu.VMEM((tt, d), BF16),
    ]
    return pl.pallas_call(
        functools.partial(_mixer_kernel, rg=rg, gm=gm, xa=xa, chunk=chunk, n_t=n_t),
        grid=(total + 1,),
        in_specs=in_specs,
        out_specs=out_specs,
        out_shape=out_shape,
        scratch_shapes=scratch,
        compiler_params=pltpu.CompilerParams(
            dimension_semantics=("arbitrary",), vmem_limit_bytes=VMEM_LIMIT_BYTES),
        name="mixer",
    )(x.reshape(n, d), *[prm[c] for c in const_names], k, v, *[prm[c] for c in const_names2])


def _dispatch_kernel(zs_ref, zl_ref, na_ref, dest_ref, hnp_ref, xs_ref, zb, sem, zsem, *, tile, lag, tm, s_rows):
    step = pl.program_id(0)
    n_exp = zs_ref.shape[0]
    nb = xs_ref.shape[0] // (tm * s_rows)

    def for_each_zero_copy(act):
        def per_expert(e, carry):
            off = zs_ref[e]
            length = zl_ref[e]
            p = tm // 2
            while p >= 1:
                bit = length & p

                @pl.when(bit != 0)
                def _(off=off, p=p):
                    dst = xs_ref.at[pl.ds(pl.multiple_of(off * s_rows, s_rows), p * s_rows)]
                    act(pltpu.make_async_copy(zb.at[pl.ds(0, p * s_rows)], dst, zsem))
                off = off + bit
                p //= 2
            return carry
        lax.fori_loop(0, n_exp, per_expert, 0)

        def per_block(b, carry):
            dst = xs_ref.at[pl.ds(pl.multiple_of(b * (tm * s_rows), tm * s_rows), tm * s_rows)]
            act(pltpu.make_async_copy(zb, dst, zsem))
            return carry
        lax.fori_loop(na_ref[0], nb, per_block, 0)

    @pl.when(step == 0)
    def _():
        zb[...] = jnp.zeros(zb.shape, U32)
        for_each_zero_copy(lambda c: c.start())

    def row_copy(n, dst_row):
        src = hnp_ref.at[pl.ds(pl.multiple_of(n * s_rows, s_rows), s_rows)]
        return pltpu.make_async_copy(src, xs_ref.at[pl.ds(pl.multiple_of(dst_row, s_rows), s_rows)], sem)

    def wait_token():
        for _ in range(TOP_K):
            row_copy(0, 0).wait()

    def body(n, carry):
        for k in range(TOP_K):
            row_copy(n, dest_ref[k * tile + n]).start(priority=k % 2)

        @pl.when(n >= lag)
        def _():
            wait_token()
        return carry

    lax.fori_loop(0, tile, body, 0)

    def drain(_, carry):
        wait_token()
        return carry

    lax.fori_loop(0, lag, drain, 0)

    @pl.when(step == 0)
    def _():
        for_each_zero_copy(lambda c: c.wait())


def _dispatch(zero_start, zero_len, n_active, dest_rows_tiles, hnp, rows, *, tile, tm, s_rows):
    n = dest_rows_tiles.shape[0] // TOP_K
    lag = min(DISPATCH_LAG, tile)
    grid_spec = pltpu.PrefetchScalarGridSpec(
        num_scalar_prefetch=3,
        grid=(n // tile,),
        in_specs=[
            pl.BlockSpec((TOP_K * tile,), lambda i, *_: (i,), memory_space=pltpu.SMEM),
            pl.BlockSpec((tile * s_rows, LANES), lambda i, *_: (i, 0)),
        ],
        out_specs=pl.BlockSpec(memory_space=pl.ANY),
        scratch_shapes=[pltpu.VMEM((tm * s_rows, LANES), U32), pltpu.SemaphoreType.DMA(()),
                        pltpu.SemaphoreType.DMA(())],
    )
    return pl.pallas_call(
        functools.partial(_dispatch_kernel, tile=tile, lag=lag, tm=tm, s_rows=s_rows),
        grid_spec=grid_spec,
        out_shape=jax.ShapeDtypeStruct((rows * s_rows, LANES), U32),
        compiler_params=pltpu.CompilerParams(
            dimension_semantics=("arbitrary",), vmem_limit_bytes=VMEM_LIMIT_BYTES,
            disable_bounds_checks=True, has_side_effects=True),
        name="dispatch",
    )(zero_start, zero_len, n_active, dest_rows_tiles, hnp)


def _expert_changed(be_ref, i):
    prev = be_ref[jnp.maximum(i - 1, 0)]
    return (i == 0) | (be_ref[i] != prev)


def _weight_ring_step(w_hbm, wbuf, wsem, be_ref, nxt_ref, rid_ref, nr_ref, i, j, n_j):
    col_tile = wbuf.shape[2]
    run = j * nr_ref[0] + rid_ref[i]
    slot = run & 1

    def tile_copy(e, jj, sl):
        cols = pl.ds(pl.multiple_of(jj * col_tile, col_tile), col_tile)
        return pltpu.make_async_copy(w_hbm.at[e, :, cols], wbuf.at[sl], wsem.at[sl])

    @pl.when(run == 0)
    def _():
        tile_copy(be_ref[0], 0, 0).start()

    tile_copy(be_ref[i], j, slot).wait()
    nxt = nxt_ref[i]

    @pl.when(nxt >= 0)
    def _():
        tile_copy(nxt, j, 1 - slot).start()

    @pl.when((nxt < 0) & (j + 1 < n_j))
    def _():
        tile_copy(be_ref[0], j + 1, 1 - slot).start()

    return slot


def _for_filled_rows(active, nq, tm, body):
    quantum = min(MOE_QUANTUM, tm)
    for q in range(1, tm // quantum + 1):
        @pl.when(active & (nq == q))
        def _(q=q):
            body(q * quantum)


def _moe_up_kernel(be_ref, na_ref, nxt_ref, rid_ref, nr_ref, nq_ref, xs_ref, w_hbm, bg_ref, bu_ref, perm_ref,
                   act_ref, wbuf, wsem, wp_s, xb_s):
    j = pl.program_id(0)
    i = pl.program_id(1)
    tn = wbuf.shape[2]
    hn_ = tn // 2
    d = xb_s.shape[1]
    active = i < na_ref[0]

    @pl.when(active & _expert_changed(be_ref, i))
    def _():
        slot = _weight_ring_step(w_hbm, wbuf, wsem, be_ref, nxt_ref, rid_ref, nr_ref, i, j, pl.num_programs(0))
        for g in range(tn // MXU_DIM):
            wg = wbuf[slot, :, g * MXU_DIM:(g + 1) * MXU_DIM].astype(BF16)
            wq = jnp.dot(wg, perm_ref[...], preferred_element_type=F32).astype(BF16)
            hw = MXU_DIM // 2
            wp_s[:, g * hw:(g + 1) * hw] = wq[:, 0:hw]
            wp_s[:, hn_ + g * hw:hn_ + (g + 1) * hw] = wq[:, hw:MXU_DIM]

    tm = xb_s.shape[0]

    def filled(rows):
        half = d // 2
        s_rows = half // LANES
        for c in range(s_rows):
            lo, hi = _load_packed_rows(xs_ref, c, rows, s_rows)
            xb_s[0:rows, c * LANES:(c + 1) * LANES] = lo.astype(BF16)
            xb_s[0:rows, half + c * LANES:half + (c + 1) * LANES] = hi.astype(BF16)
        gu = jnp.dot(xb_s[0:rows, :], wp_s[...], preferred_element_type=F32)
        gate = jnp.minimum(gu[:, 0:hn_] + bg_ref[0], SWIGLU_LIMIT)
        up = jnp.clip(gu[:, hn_:tn] + bu_ref[0], -SWIGLU_LIMIT, SWIGLU_LIMIT)
        act_ref[0:rows, :] = ((up + 1.0) * (gate * _sigmoid(SWIGLU_ALPHA * gate))).astype(BF16)
        if rows < tm:
            act_ref[rows:tm, :] = jnp.zeros((tm - rows, act_ref.shape[1]), BF16)

    _for_filled_rows(active, nq_ref[i], tm, filled)

    @pl.when(jnp.logical_not(active))
    def _():
        act_ref[...] = jnp.zeros(act_ref.shape, BF16)


def _moe_up(plan, xs, w_gate_up, bg, bu, perm, *, tm, tn):
    n_exp, d, f2 = w_gate_up.shape
    s_rows = d // (2 * LANES)
    rows = xs.shape[0] // s_rows
    nb = rows // tm

    def blk(i, na):
        return jnp.minimum(i, na[0] - 1)

    grid_spec = pltpu.PrefetchScalarGridSpec(
        num_scalar_prefetch=len(plan),
        grid=(f2 // tn, nb),
        in_specs=[
            pl.BlockSpec((tm * s_rows, LANES), lambda j, i, be, na, *_: (blk(i, na), 0)),
            pl.BlockSpec(memory_space=pl.ANY),
            pl.BlockSpec((1, 1, tn // 2), lambda j, i, be, na, *_: (be[blk(i, na)], 0, j)),
            pl.BlockSpec((1, 1, tn // 2), lambda j, i, be, na, *_: (be[blk(i, na)], 0, j)),
            pl.BlockSpec((MXU_DIM, MXU_DIM), lambda j, i, *_: (0, 0)),
        ],
        out_specs=pl.BlockSpec((tm, tn // 2), lambda j, i, *_: (i, j)),
        scratch_shapes=[pltpu.VMEM((2, d, tn), F32), pltpu.SemaphoreType.DMA((2,)),
                        pltpu.VMEM((d, tn), BF16), pltpu.VMEM((tm, d), BF16)],
    )
    return pl.pallas_call(
        _moe_up_kernel,
        grid_spec=grid_spec,
        out_shape=jax.ShapeDtypeStruct((rows, f2 // 2), BF16),
        compiler_params=pltpu.CompilerParams(
            dimension_semantics=("arbitrary", "arbitrary"), vmem_limit_bytes=VMEM_LIMIT_BYTES),
        name="moe_up",
    )(*plan, xs, w_gate_up, bg, bu, perm)


def _moe_down_kernel(be_ref, na_ref, nxt_ref, rid_ref, nr_ref, nq_ref, act_ref, w_hbm, b_ref, ys_ref,
                     wbuf, wsem, wb_s):
    i = pl.program_id(0)
    active = i < na_ref[0]
    tm = act_ref.shape[0]
    s_rows = wb_s.shape[1] // (2 * LANES)

    @pl.when(active & _expert_changed(be_ref, i))
    def _():
        slot = _weight_ring_step(w_hbm, wbuf, wsem, be_ref, nxt_ref, rid_ref, nr_ref, i, 0, 1)
        wb_s[...] = wbuf[slot].astype(BF16)

    def filled(rows):
        y = jnp.dot(act_ref[0:rows, :], wb_s[...], preferred_element_type=F32) + b_ref[0]
        _store_packed_rows(ys_ref, y, s_rows)
        if rows < tm:
            ys_ref[rows * s_rows:tm * s_rows, :] = jnp.zeros(((tm - rows) * s_rows, LANES), U32)

    _for_filled_rows(active, nq_ref[i], tm, filled)

    @pl.when(jnp.logical_not(active))
    def _():
        ys_ref[...] = jnp.zeros(ys_ref.shape, U32)


def _moe_down(plan, act, w_down, b_down, *, tm):
    rows, f = act.shape
    n_exp, _, d = w_down.shape
    s_rows = d // (2 * LANES)

    def blk(i, na):
        return jnp.minimum(i, na[0] - 1)

    grid_spec = pltpu.PrefetchScalarGridSpec(
        num_scalar_prefetch=len(plan),
        grid=(rows // tm,),
        in_specs=[
            pl.BlockSpec((tm, f), lambda i, be, na, *_: (blk(i, na), 0)),
            pl.BlockSpec(memory_space=pl.ANY),
            pl.BlockSpec((1, 1, d), lambda i, be, na, *_: (be[blk(i, na)], 0, 0)),
        ],
        out_specs=pl.BlockSpec((tm * s_rows, LANES), lambda i, *_: (i, 0)),
        scratch_shapes=[pltpu.VMEM((2, f, d), F32), pltpu.SemaphoreType.DMA((2,)), pltpu.VMEM((f, d), BF16)],
    )
    return pl.pallas_call(
        _moe_down_kernel,
        grid_spec=grid_spec,
        out_shape=jax.ShapeDtypeStruct((rows * s_rows, LANES), U32),
        compiler_params=pltpu.CompilerParams(
            dimension_semantics=("arbitrary",), vmem_limit_bytes=VMEM_LIMIT_BYTES),
        name="moe_down",
    )(*plan, act, w_down, b_down)


def _combine_kernel(dcur_ref, dnext_ref, ys_ref, h_ref, gates_ref, fg_ref, out_ref, buf, sem, *, tile, s_rows):
    s = pl.program_id(0)
    n_steps = pl.num_programs(0)

    slot_rows = TOP_K * tile * s_rows

    def row_copy(slot, k, n, src_row):
        src = ys_ref.at[pl.ds(pl.multiple_of(src_row, s_rows), s_rows)]
        dst = buf.at[slot, pl.ds(pl.multiple_of((k * tile + n) * s_rows, s_rows), s_rows)]
        return pltpu.make_async_copy(src, dst, sem.at[slot])

    def issue(d_ref, slot):
        def body(n, carry):
            for k in range(TOP_K):
                row_copy(slot, k, n, d_ref[k * tile + n]).start(priority=k % 2)
            return carry
        lax.fori_loop(0, tile, body, 0, unroll=ISSUE_UNROLL)

    @pl.when(s == 0)
    def _():
        issue(dcur_ref, 0)

    @pl.when(s + 1 < n_steps)
    def _():
        issue(dnext_ref, (s + 1) % 2)

    def consume(slot):
        pltpu.make_async_copy(ys_ref.at[pl.ds(0, slot_rows)], buf.at[slot], sem.at[slot]).wait()
        gs = [gates_ref[:, k:k + 1] for k in range(TOP_K)]
        d = out_ref.shape[1]
        half = d // 2
        ssq = jnp.zeros((tile, 1), F32)
        for c in range(s_rows):
            lo_cols = slice(c * LANES, (c + 1) * LANES)
            hi_cols = slice(half + c * LANES, half + (c + 1) * LANES)
            acc_lo = h_ref[:, lo_cols]
            acc_hi = h_ref[:, hi_cols]
            for k in range(TOP_K):
                lo, hi = _load_packed_rows(buf.at[slot], c, tile, s_rows, row0=k * tile)
                acc_lo = acc_lo + gs[k] * lo
                acc_hi = acc_hi + gs[k] * hi
            out_ref[:, lo_cols] = acc_lo
            out_ref[:, hi_cols] = acc_hi
            ssq = ssq + jnp.sum(acc_lo * acc_lo + acc_hi * acc_hi, axis=-1, keepdims=True)
        out_ref[...] = out_ref[...] * lax.rsqrt(ssq / d + EPS) * fg_ref[...]

    for slot in range(2):
        @pl.when(s % 2 == slot)
        def _(slot=slot):
            consume(slot)


def _combine(src_rows_tiles, ys, h2, gates_t, fg, *, n, tile):
    d = h2.shape[1]
    s_rows = d // (2 * LANES)
    n_steps = n // tile
    return pl.pallas_call(
        functools.partial(_combine_kernel, tile=tile, s_rows=s_rows),
        grid=(n_steps,),
        in_specs=[
            pl.BlockSpec((TOP_K * tile,), lambda i: (i,), memory_space=pltpu.SMEM),
            pl.BlockSpec((TOP_K * tile,), lambda i: (jnp.minimum(i + 1, n_steps - 1),), memory_space=pltpu.SMEM),
            pl.BlockSpec(memory_space=pl.ANY),
            pl.BlockSpec((tile, d), lambda i: (i, 0)),
            pl.BlockSpec((tile, TOP_K), lambda i: (i, 0)),
            _const_spec((1, d)),
        ],
        out_specs=pl.BlockSpec((tile, d), lambda i: (i, 0)),
        out_shape=jax.ShapeDtypeStruct((n, d), F32),
        scratch_shapes=[pltpu.VMEM((2, TOP_K * tile * s_rows, LANES), U32), pltpu.SemaphoreType.DMA((2,))],
        compiler_params=pltpu.CompilerParams(
            dimension_semantics=("arbitrary",), vmem_limit_bytes=VMEM_LIMIT_BYTES,
            disable_bounds_checks=True),
        name="combine",
    )(src_rows_tiles, src_rows_tiles, ys, h2, gates_t, fg)


def _tile_major(a, tile):
    k, n = a.shape
    return a.reshape(k, n // tile, tile).transpose(1, 0, 2).reshape(-1)


def _layer(h, mem, ln_mix_g, w_in, conv_w, conv_b, w_rg_a, b_rg_a, w_rg_i, b_rg_i, lru_lambda, gm_v_norm_g,
           w_spatial, b_spatial, mem_norm_g, w_mem_k, w_mem_v, out_norm_g, w_out, ln_ffn_g, w_router, b_router,
           w_gate_up, b_gate_up, w_down, b_down, final_g, *, mix_tile, moe_tile, up_tn,
           dispatch_tile, combine_tile):
    b, t, d = h.shape
    n = b * t
    n_exp = w_router.shape[1]
    row = lambda a: a.reshape(1, -1)
    km, v = _mem_kv(mem, row(mem_norm_g), w_mem_k.astype(BF16), w_mem_v.astype(BF16))
    heads, hd, _ = w_rg_a.shape
    per = max(1, min(heads, MXU_DIM // hd))
    eye = jnp.eye(per, dtype=w_rg_a.dtype)

    def group_block_diag(w):
        wg = w.reshape(heads // per, per, hd, hd)
        return (eye[None, :, None, :, None] * wg[:, :, :, None, :]).reshape(heads // per, per * hd, per * hd)

    prm = dict(
        lnmix=row(ln_mix_g), w_in=w_in.astype(BF16), conv_w=conv_w, conv_b=row(conv_b),
        wai=jnp.concatenate([group_block_diag(w_rg_a), group_block_diag(w_rg_i)], axis=2).astype(BF16),
        ba=row(b_rg_a), bi=row(b_rg_i), lam=row(lru_lambda),
        gvn=row(gm_v_norm_g), wsp=w_spatial, bst=b_spatial.T, outg=row(out_norm_g), w_out=w_out.astype(BF16),
        lnffn=row(ln_ffn_g), wrt=w_router.T.astype(BF16), br=b_router.reshape(-1, 1))
    h1, hnp, idx, gates, rank, cnt = _mixer(h, km, v, prm, tt=mix_tile)

    counts = cnt[:, 0].astype(I32)
    padded = ((counts + moe_tile - 1) // moe_tile) * moe_tile
    pad_end = jnp.cumsum(padded)
    pad_start = pad_end - padded
    s_rows = d // (2 * LANES)
    experts = jnp.arange(n_exp, dtype=I32)
    start_of = jnp.sum(jnp.where(idx[..., None] == experts, pad_start, 0), axis=-1)
    dest_rows = (start_of + rank) * s_rows
    nb = (n * TOP_K) // moe_tile + n_exp
    n_active = (pad_end[-1] // moe_tile).astype(I32).reshape(1)
    block_first = jnp.arange(nb, dtype=I32) * moe_tile
    block_expert = jnp.minimum(
        jnp.sum((pad_end[None, :] <= block_first[:, None]).astype(I32), axis=-1), n_exp - 1)
    blocks = jnp.arange(nb, dtype=I32)
    starts_run = (blocks < n_active[0]) & ((blocks == 0) | (block_expert != jnp.roll(block_expert, 1)))
    run_id = jnp.cumsum(starts_run.astype(I32)) - 1
    later_start = lax.cummin(jnp.where(starts_run, blocks, nb)[::-1])[::-1]
    next_start = jnp.concatenate([later_start[1:], jnp.full((1,), nb, I32)])
    next_expert = jnp.where(next_start < nb, block_expert[jnp.minimum(next_start, nb - 1)], -1).astype(I32)
    n_runs = jnp.sum(starts_run.astype(I32)).reshape(1)
    filled_rows = jnp.clip(pad_start[block_expert] + counts[block_expert] - block_first, 1, moe_tile)
    quantum = min(MOE_QUANTUM, moe_tile)
    n_quanta = ((filled_rows + quantum - 1) // quantum).astype(I32)
    plan = (block_expert, n_active, next_expert, run_id.astype(I32), n_runs, n_quanta)

    xs = _dispatch(pad_start + counts, padded - counts, n_active, _tile_major(dest_rows, dispatch_tile), hnp,
                   nb * moe_tile, tile=dispatch_tile, tm=moe_tile, s_rows=s_rows)
    f = w_down.shape[1]
    r_ = jnp.arange(MXU_DIM)
    src = jnp.where(r_ < MXU_DIM // 2, 2 * r_, 2 * (r_ - MXU_DIM // 2) + 1)
    perm = (jnp.arange(MXU_DIM)[:, None] == src[None, :]).astype(BF16)
    bg = b_gate_up[:, 0::2].reshape(n_exp, 1, f)
    bu = b_gate_up[:, 1::2].reshape(n_exp, 1, f)
    act = _moe_up(plan, xs, w_gate_up, bg, bu, perm, tm=moe_tile, tn=up_tn)
    ys = _moe_down(plan, act, w_down, b_down.reshape(n_exp, 1, d), tm=moe_tile)
    out = _combine(_tile_major(dest_rows, combine_tile), ys, h1, gates.T, row(final_g), n=n, tile=combine_tile)
    return out.reshape(b, t, d)


def kernel(x, mem, ln_mix_g, w_in, conv_w, conv_b, w_rg_a, b_rg_a, w_rg_i, b_rg_i, lru_lambda, gm_v_norm_g, w_spatial, b_spatial, mem_norm_g, w_mem_k, w_mem_v, out_norm_g, w_out, ln_ffn_g, w_router, b_router, w_gate_up, b_gate_up, w_down, b_down, final_norm_g):
    depth = w_in.shape[0]
    assert depth == 1, "the final RMSNorm is fused into the single layer's combine stage"
    return _layer(
        x, mem, ln_mix_g[0], w_in[0], conv_w[0], conv_b[0], w_rg_a[0], b_rg_a[0], w_rg_i[0], b_rg_i[0],
        lru_lambda[0], gm_v_norm_g[0], w_spatial[0], b_spatial[0], mem_norm_g[0], w_mem_k[0], w_mem_v[0],
        out_norm_g[0], w_out[0], ln_ffn_g[0], w_router[0], b_router[0], w_gate_up[0], b_gate_up[0], w_down[0],
        b_down[0], final_norm_g,
        mix_tile=MIX_TILE, moe_tile=MOE_TILE, up_tn=UP_TN,
        dispatch_tile=DISPATCH_TILE, combine_tile=COMBINE_TILE)
```
